```python
import jax
import jax.numpy as jnp
from jax import lax
import numpy as np

D_MODEL = 2048
BATCH = 2
SEQ = 8192
DEPTH = 1
DEC_BATCH = 128
DEC_SEQ = 4
PAST_LEN = 16384
PAGE_SIZE = 128

N_Q_HEADS = 16
N_KV_HEADS = 4
GROUP = N_Q_HEADS // N_KV_HEADS
HEAD_DIM = 64
WINDOW = 128
N_DN_HEADS = 16
DK = 128
DV = 128
CONV_W = 4
DN_CHUNK = 64
CONV_DIM = N_DN_HEADS * (2 * DK + DV)
N_EXPERTS = 32
TOP_K = 4
D_FF = D_MODEL
SWIGLU_LIMIT = 7.0
SWIGLU_ALPHA = 1.702
MOE_BLOCK = 128
PLE_DIM = 256
EPS = 1e-6

ATT_Q = N_Q_HEADS * HEAD_DIM
ATT_KV = N_KV_HEADS * HEAD_DIM
DN_V = N_DN_HEADS * DV
IN_SPLITS = (ATT_Q, ATT_KV, ATT_KV, CONV_DIM, DN_V, N_DN_HEADS, N_DN_HEADS, D_MODEL, D_MODEL)
IN_DIM = sum(IN_SPLITS)

kernel_name = 'hybrid_swa_sink_gdn_moe_step'


def rmsnorm(x, w):
    xf = x.astype(jnp.float32)
    y = xf * lax.rsqrt(jnp.mean(xf * xf, axis=-1, keepdims=True) + EPS)
    return (y * w.astype(jnp.float32)).astype(x.dtype)


def l2norm(x):
    xf = x.astype(jnp.float32)
    return xf * lax.rsqrt(jnp.sum(xf * xf, axis=-1, keepdims=True) + EPS)


def sink_attention(q, k, v, mask, sinks):
    s = jnp.einsum('...qhgd,...khd->...hgqk', q, k).astype(jnp.float32) * (HEAD_DIM ** -0.5)
    s = jnp.where(mask, s, -jnp.inf)
    sink = jnp.broadcast_to(sinks.astype(jnp.float32).reshape(N_KV_HEADS, GROUP, 1, 1), s.shape[:-1] + (1,))
    prob = jax.nn.softmax(jnp.concatenate([s, sink], axis=-1), axis=-1)[..., :-1]
    return jnp.einsum('...hgqk,...khd->...qhgd', prob.astype(v.dtype), v)


def swa_prompt(q, k, v, sinks):
    b, seq = q.shape[:2]
    nb = seq // WINDOW
    qb = q.reshape(b, nb, WINDOW, N_KV_HEADS, GROUP, HEAD_DIM)
    kb = k.reshape(b, nb, WINDOW, N_KV_HEADS, HEAD_DIM)
    vb = v.reshape(b, nb, WINDOW, N_KV_HEADS, HEAD_DIM)
    shift = ((0, 0), (1, 0), (0, 0), (0, 0), (0, 0))
    k2 = jnp.concatenate([jnp.pad(kb, shift)[:, :-1], kb], axis=2)
    v2 = jnp.concatenate([jnp.pad(vb, shift)[:, :-1], vb], axis=2)
    qi = jnp.arange(WINDOW)[:, None]
    km = jnp.arange(2 * WINDOW)[None, :]
    band = (km >= qi) & (km <= WINDOW + qi)
    first = (jnp.arange(nb) > 0)[:, None, None]
    mask = band[None] & (first | (km >= WINDOW)[None])
    o = sink_attention(qb, k2, v2, mask[:, None, None], sinks)
    return o.reshape(b, seq, ATT_Q)


def swa_sample(q, k, v, k_past, v_past, sinks):
    b, t = q.shape[:2]
    wc = k_past.shape[1]
    kc = jnp.concatenate([k_past, k], axis=1)
    vc = jnp.concatenate([v_past, v], axis=1)
    qi = jnp.arange(t)[:, None]
    km = jnp.arange(wc + t)[None, :]
    mask = (km <= wc + qi) & (km >= wc + qi - WINDOW)
    o = sink_attention(q, kc, vc, mask, sinks)
    return o.reshape(b, t, ATT_Q), kc[:, -wc:], vc[:, -wc:]


def causal_conv(xc, past, w):
    seq = xc.shape[1]
    xp = jnp.concatenate([past, xc], axis=1)
    y = sum(xp[:, j:j + seq] * w[j] for j in range(CONV_W))
    return jax.nn.silu(y), xp[:, -(CONV_W - 1):]


def gated_delta_chunked(q, k, v, g, beta, s0):
    b, seq, h, _ = k.shape
    c = DN_CHUNK if seq >= DN_CHUNK else seq
    pad = (-seq) % c

    def prep(t):
        t = jnp.pad(t.astype(jnp.float32), [(0, 0), (0, pad)] + [(0, 0)] * (t.ndim - 2))
        t = jnp.moveaxis(t, 2, 1)
        return t.reshape(t.shape[:2] + (-1, c) + t.shape[3:])

    qc, kc, vc, gc, bc = prep(q), prep(k), prep(v), prep(g), prep(beta)
    gcum = jnp.cumsum(gc, axis=-1)
    incl = jnp.tril(jnp.ones((c, c), dtype=bool))
    strict = jnp.tril(jnp.ones((c, c), dtype=bool), k=-1)
    diff = gcum[..., :, None] - gcum[..., None, :]
    decay = jnp.where(incl, jnp.exp(jnp.where(incl, diff, 0.0)), 0.0)
    kb = kc * bc[..., None]
    vb = vc * bc[..., None]
    a_mat = jnp.where(strict, jnp.einsum('bhncd,bhnsd->bhncs', kb, kc) * decay, 0.0)
    eye = jnp.eye(c, dtype=jnp.float32)
    t_inv = lax.linalg.triangular_solve(eye + a_mat, jnp.broadcast_to(eye, a_mat.shape),
                                        left_side=True, lower=True, unit_diagonal=True)
    u0 = t_inv @ vb
    w = t_inv @ (kb * jnp.exp(gcum)[..., None])
    qk = jnp.where(incl, jnp.einsum('bhncd,bhnsd->bhncs', qc, kc) * decay, 0.0)
    qexp = qc * jnp.exp(gcum)[..., None]
    glast = gcum[..., -1]
    kend = kc * jnp.exp(glast[..., None] - gcum)[..., None]
    xs = tuple(jnp.moveaxis(t, 2, 0) for t in (w, u0, qk, qexp, kend, glast))

    def step(s, xs_c):
        w_c, u0_c, qk_c, qexp_c, kend_c, gl_c = xs_c
        u = u0_c - w_c @ s
        o = qexp_c @ s + qk_c @ u
        s = s * jnp.exp(gl_c)[..., None, None] + jnp.einsum('bhcd,bhce->bhde', kend_c, u)
        return s, o

    s_fin, o = lax.scan(step, s0.astype(jnp.float32), xs)
    o = jnp.moveaxis(o, 0, 2).reshape(b, h, -1, DV)[:, :, :seq]
    return jnp.moveaxis(o, 1, 2).astype(v.dtype), s_fin.astype(s0.dtype)


def deltanet_branch(qkv, z, b_in, a_in, conv_past, s0, conv_w, a_log, dt_bias, dn_norm):
    b, seq = qkv.shape[:2]
    qkv_c, new_conv = causal_conv(qkv, conv_past, conv_w)
    qd, kd, vd = jnp.split(qkv_c, [N_DN_HEADS * DK, 2 * N_DN_HEADS * DK], axis=-1)
    qd = l2norm(qd.reshape(b, seq, N_DN_HEADS, DK)) * (DK ** -0.5)
    kd = l2norm(kd.reshape(b, seq, N_DN_HEADS, DK))
    vd = vd.reshape(b, seq, N_DN_HEADS, DV)
    beta = jax.nn.sigmoid(b_in.astype(jnp.float32))
    g = -jnp.exp(a_log.astype(jnp.float32)) * jax.nn.softplus(a_in.astype(jnp.float32) + dt_bias.astype(jnp.float32))
    o, s_new = gated_delta_chunked(qd, kd, vd, g, beta, s0)
    o = rmsnorm(o, dn_norm) * jax.nn.silu(z.reshape(b, seq, N_DN_HEADS, DV))
    return o.reshape(b, seq, DN_V), new_conv, s_new


def routed_experts(h, w_router, b_router, w_exp_in, b_exp_in, w_exp_out, b_exp_out):
    lead = h.shape[:-1]
    xf = h.reshape(-1, D_MODEL)
    m = xf.shape[0]
    n_assign = m * TOP_K
    logits = xf.astype(jnp.float32) @ w_router.astype(jnp.float32) + b_router.astype(jnp.float32)
    top_val, top_idx = lax.top_k(logits, TOP_K)
    gates = jax.nn.softmax(top_val, axis=-1)
    flat_e = top_idx.reshape(-1).astype(jnp.int32)
    order = jnp.argsort(flat_e)
    sorted_e = flat_e[order]
    tok = (order // TOP_K).astype(jnp.int32)
    counts = jax.ops.segment_sum(jnp.ones_like(flat_e), flat_e, num_segments=N_EXPERTS)
    padded = (counts + MOE_BLOCK - 1) // MOE_BLOCK * MOE_BLOCK
    pad_end = jnp.cumsum(padded)
    pad_start = pad_end - padded
    grp_start = jnp.cumsum(counts) - counts
    dest = pad_start[sorted_e] + jnp.arange(n_assign, dtype=jnp.int32) - grp_start[sorted_e]
    n_blocks = (n_assign + N_EXPERTS * (MOE_BLOCK - 1) + MOE_BLOCK - 1) // MOE_BLOCK
    row_tok = jnp.zeros((n_blocks * MOE_BLOCK,), jnp.int32).at[dest].set(tok)
    block_start = jnp.arange(n_blocks, dtype=jnp.int32) * MOE_BLOCK
    block_exp = jnp.minimum(jnp.searchsorted(pad_end, block_start, side='right'), N_EXPERTS - 1)

    def expert_block(args):
        rows, e = args
        xb = xf[rows]
        gu = xb @ w_exp_in[e] + b_exp_in[e]
        gate = jnp.minimum(gu[:, :D_FF], SWIGLU_LIMIT)
        up = jnp.clip(gu[:, D_FF:], -SWIGLU_LIMIT, SWIGLU_LIMIT)
        act = (up + 1.0) * gate * jax.nn.sigmoid(SWIGLU_ALPHA * gate)
        return act @ w_exp_out[e] + b_exp_out[e]

    yb = lax.map(expert_block, (row_tok.reshape(n_blocks, MOE_BLOCK), block_exp))
    y_assign = yb.reshape(-1, D_MODEL)[dest]
    w_assign = gates.reshape(-1)[order].astype(h.dtype)
    out = jax.ops.segment_sum(y_assign * w_assign[:, None], tok, num_segments=m)
    return out.reshape(lead + (D_MODEL,))


def decoder_layer(x, p, k_past, v_past, conv_past, s0, prompt_mode, lw):
    b, seq, _ = x.shape
    h = rmsnorm(x, lw['norm_mix'])
    proj = h @ lw['w_in']
    cuts = [int(c) for c in np.cumsum(IN_SPLITS)[:-1]]
    q_a, k_a, v_a, qkv_d, z_d, b_d, a_d, g_a, g_b = jnp.split(proj, cuts, axis=-1)
    q_a = q_a.reshape(b, seq, N_KV_HEADS, GROUP, HEAD_DIM)
    k_a = k_a.reshape(b, seq, N_KV_HEADS, HEAD_DIM)
    v_a = v_a.reshape(b, seq, N_KV_HEADS, HEAD_DIM)
    if prompt_mode:
        o_a = swa_prompt(q_a, k_a, v_a, lw['sinks'])
        k_new, v_new = k_a[:, -WINDOW:], v_a[:, -WINDOW:]
    else:
        o_a, k_new, v_new = swa_sample(q_a, k_a, v_a, k_past, v_past, lw['sinks'])
    o_d, conv_new, s_new = deltanet_branch(qkv_d, z_d, b_d, a_d, conv_past, s0, lw['conv_w'],
                                           lw['a_log'], lw['dt_bias'], lw['dn_norm'])
    branch_a = o_a @ lw['w_attn_branch']
    branch_b = o_d @ lw['w_dn_branch']
    merged = jax.nn.sigmoid(g_a) * branch_a + jax.nn.sigmoid(g_b) * branch_b
    x = x + merged @ lw['w_out']
    x = x + routed_experts(rmsnorm(x, lw['norm_moe']), lw['w_router'], lw['b_router'],
                           lw['w_exp_in'], lw['b_exp_in'], lw['w_exp_out'], lw['b_exp_out'])
    hp = rmsnorm(x, lw['norm_ple'])
    x = x + jax.nn.sigmoid(hp @ lw['w_ple_gate']) * (p @ lw['w_ple_proj'])
    return x, (k_new, v_new, conv_new, s_new)


def setup_inputs(seed: int = 0) -> dict:
    key = jax.random.key(seed)
    ks = jax.random.split(key, 32)
    f32 = jnp.float32
    w_cache = min(WINDOW, PAST_LEN)

    def nrm(k, shape, scale):
        return jax.random.normal(k, shape, f32) * scale

    def gain(k, shape):
        return 1.0 + 0.02 * jax.random.normal(k, shape, f32)

    dt = jax.random.uniform(ks[5], (DEPTH, N_DN_HEADS), f32, 0.001, 0.1)
    return {
        'x_prompt': nrm(ks[0], (BATCH, SEQ, D_MODEL), 1.0),
        'x_sample': nrm(ks[1], (DEC_BATCH, DEC_SEQ, D_MODEL), 1.0),
        'cache_k': nrm(ks[2], (DEPTH, DEC_BATCH, w_cache, N_KV_HEADS, HEAD_DIM), 1.0),
        'cache_v': nrm(ks[3], (DEPTH, DEC_BATCH, w_cache, N_KV_HEADS, HEAD_DIM), 1.0),
        'state_conv': nrm(ks[4], (DEPTH, DEC_BATCH, CONV_W - 1, CONV_DIM), 1.0),
        'state_ssm': nrm(ks[6], (DEPTH, DEC_BATCH, N_DN_HEADS, DK, DV), DK ** -0.5),
        'p_prompt': nrm(ks[7], (DEPTH, BATCH, SEQ, PLE_DIM), 1.0),
        'p_sample': nrm(ks[8], (DEPTH, DEC_BATCH, DEC_SEQ, PLE_DIM), 1.0),
        'w_in': nrm(ks[9], (DEPTH, D_MODEL, IN_DIM), D_MODEL ** -0.5),
        'sinks': nrm(ks[10], (DEPTH, N_Q_HEADS), 0.5),
        'conv_w': nrm(ks[11], (DEPTH, CONV_W, CONV_DIM), CONV_W ** -0.5),
        'a_log': jnp.log(jax.random.uniform(ks[12], (DEPTH, N_DN_HEADS), f32, 1.0, 16.0)),
        'dt_bias': jnp.log(jnp.expm1(dt)),
        'dn_norm': gain(ks[13], (DEPTH, DV)),
        'w_attn_branch': nrm(ks[14], (DEPTH, ATT_Q, D_MODEL), ATT_Q ** -0.5),
        'w_dn_branch': nrm(ks[15], (DEPTH, DN_V, D_MODEL), DN_V ** -0.5),
        'w_out': nrm(ks[16], (DEPTH, D_MODEL, D_MODEL), D_MODEL ** -0.5),
        'w_router': nrm(ks[17], (DEPTH, D_MODEL, N_EXPERTS), D_MODEL ** -0.5),
        'b_router': nrm(ks[18], (DEPTH, N_EXPERTS), 0.01),
        'w_exp_in': nrm(ks[19], (DEPTH, N_EXPERTS, D_MODEL, 2 * D_FF), D_MODEL ** -0.5),
        'b_exp_in': nrm(ks[20], (DEPTH, N_EXPERTS, 2 * D_FF), 0.01),
        'w_exp_out': nrm(ks[21], (DEPTH, N_EXPERTS, D_FF, D_MODEL), D_FF ** -0.5),
        'b_exp_out': nrm(ks[22], (DEPTH, N_EXPERTS, D_MODEL), 0.01),
        'w_ple_proj': nrm(ks[23], (DEPTH, PLE_DIM, D_MODEL), PLE_DIM ** -0.5),
        'w_ple_gate': nrm(ks[24], (DEPTH, D_MODEL, D_MODEL), D_MODEL ** -0.5),
        'norm_mix': gain(ks[25], (DEPTH, D_MODEL)),
        'norm_moe': gain(ks[26], (DEPTH, D_MODEL)),
        'norm_ple': gain(ks[27], (DEPTH, D_MODEL)),
        'norm_final': gain(ks[28], (D_MODEL,)),
    }


def reference(x_prompt, x_sample, cache_k, cache_v, state_conv, state_ssm, p_prompt, p_sample,
              w_in, sinks, conv_w, a_log, dt_bias, dn_norm, w_attn_branch, w_dn_branch, w_out,
              w_router, b_router, w_exp_in, b_exp_in, w_exp_out, b_exp_out, w_ple_proj, w_ple_gate,
              norm_mix, norm_moe, norm_ple, norm_final):
    xp, xs = x_prompt, x_sample
    bp = x_prompt.shape[0]
    states_p, states_s = [], []
    for i in range(DEPTH):
        lw = {
            'w_in': w_in[i], 'sinks': sinks[i], 'conv_w': conv_w[i], 'a_log': a_log[i],
            'dt_bias': dt_bias[i], 'dn_norm': dn_norm[i], 'w_attn_branch': w_attn_branch[i],
            'w_dn_branch': w_dn_branch[i], 'w_out': w_out[i], 'w_router': w_router[i],
            'b_router': b_router[i], 'w_exp_in': w_exp_in[i], 'b_exp_in': b_exp_in[i],
            'w_exp_out': w_exp_out[i], 'b_exp_out': b_exp_out[i], 'w_ple_proj': w_ple_proj[i],
            'w_ple_gate': w_ple_gate[i], 'norm_mix': norm_mix[i], 'norm_moe': norm_moe[i],
            'norm_ple': norm_ple[i],
        }
        conv0 = jnp.zeros((bp, CONV_W - 1, CONV_DIM), x_prompt.dtype)
        s00 = jnp.zeros((bp, N_DN_HEADS, DK, DV), state_ssm.dtype)
        xp, st_p = decoder_layer(xp, p_prompt[i], None, None, conv0, s00, True, lw)
        xs, st_s = decoder_layer(xs, p_sample[i], cache_k[i], cache_v[i], state_conv[i], state_ssm[i], False, lw)
        states_p.append(st_p)
        states_s.append(st_s)
    y_prompt = rmsnorm(xp, norm_final)
    y_sample = rmsnorm(xs, norm_final)
    new_k_prompt = jnp.stack([st[0] for st in states_p])
    new_v_prompt = jnp.stack([st[1] for st in states_p])
    new_conv_prompt = jnp.stack([st[2] for st in states_p])
    new_ssm_prompt = jnp.stack([st[3] for st in states_p])
    new_k_sample = jnp.stack([st[0] for st in states_s])
    new_v_sample = jnp.stack([st[1] for st in states_s])
    new_conv_sample = jnp.stack([st[2] for st in states_s])
    new_ssm_sample = jnp.stack([st[3] for st in states_s])
    return (y_prompt, y_sample, new_k_prompt, new_v_prompt, new_conv_prompt, new_ssm_prompt,
            new_k_sample, new_v_sample, new_conv_sample, new_ssm_sample)
```

```python
import functools

import jax
import jax.numpy as jnp
from jax import lax
from jax.experimental import pallas as pl
from jax.experimental.pallas import tpu as pltpu

F32 = jnp.float32
BF16 = jnp.bfloat16
I32 = jnp.int32

D_MODEL = 2048
N_Q_HEADS = 16
N_KV_HEADS = 4
GROUP = N_Q_HEADS // N_KV_HEADS
HEAD_DIM = 64
WINDOW = 128
N_DN_HEADS = 16
DK = 128
DV = 128
CONV_W = 4
CONV_DIM = N_DN_HEADS * (2 * DK + DV)
N_EXPERTS = 32
TOP_K = 4
D_FF = D_MODEL
SWIGLU_LIMIT = 7.0
SWIGLU_ALPHA = 1.702
EPS = 1e-6
ATT_Q = N_Q_HEADS * HEAD_DIM
ATT_KV = N_KV_HEADS * HEAD_DIM
DN_V = N_DN_HEADS * DV

COL_QKV_D = ATT_Q + 2 * ATT_KV
COL_Z = COL_QKV_D + CONV_DIM
COL_BA = COL_Z + DN_V
COL_GATES = COL_BA + 2 * N_DN_HEADS

LANES = 128
SUBLANES = 8
VMEM_LIMIT_BYTES = 56 * 2 ** 20

GDN_CHUNK = 64
GDN_TBLOCK = 256
GDN_HG = 4
MOE_ROWS = 1024
MOE_SUB = 256
MOE_TF = 256
TOK_TILE = 256
NEG_BIG = -1e30


def _cp(sem, vmem=VMEM_LIMIT_BYTES):
    return pltpu.CompilerParams(dimension_semantics=sem, vmem_limit_bytes=vmem)


def _sigmoid(x):
    return 1.0 / (1.0 + jnp.exp(-x))


def _softplus(x):
    return jnp.maximum(x, 0.0) + jnp.log1p(jnp.exp(-jnp.abs(x)))


def _bdot(a, b):
    return jnp.dot(a.astype(BF16), b.astype(BF16), preferred_element_type=F32)


def _bdot_nt(a, b):
    return lax.dot_general(a.astype(BF16), b.astype(BF16), (((1,), (1,)), ((), ())),
                           preferred_element_type=F32)


def _bdot_tn(a, b):
    return lax.dot_general(a.astype(BF16), b.astype(BF16), (((0,), (0,)), ((), ())),
                           preferred_element_type=F32)


def _fdot(a, b):
    return jnp.dot(a, b, preferred_element_type=F32, precision=lax.Precision.HIGHEST)


def _rms_kernel(x_ref, g_ref, o_ref):
    x = x_ref[...]
    y = x * lax.rsqrt(jnp.mean(x * x, axis=-1, keepdims=True) + EPS)
    o_ref[...] = (y * g_ref[...]).astype(o_ref.dtype)


def rms_rows(x, gamma, out_dtype, tm=512):
    m, d = x.shape
    return pl.pallas_call(
        _rms_kernel,
        grid=(m // tm,),
        in_specs=[pl.BlockSpec((tm, d), lambda i: (i, 0)), pl.BlockSpec((1, d), lambda i: (0, 0))],
        out_specs=pl.BlockSpec((tm, d), lambda i: (i, 0)),
        out_shape=jax.ShapeDtypeStruct((m, d), out_dtype),
        compiler_params=_cp(("parallel",)),
        name="rms_rows",
    )(x, gamma.reshape(1, d))


def _fused_mm_kernel(*refs, n_pairs, n_extra, epilogue):
    a_refs = refs[:n_pairs]
    w_refs = refs[n_pairs:2 * n_pairs]
    e_refs = refs[2 * n_pairs:2 * n_pairs + n_extra]
    o_ref = refs[2 * n_pairs + n_extra]
    wb_refs = refs[2 * n_pairs + n_extra + 1:]

    @pl.when(pl.program_id(1) == 0)
    def _():
        for w_ref, wb_ref in zip(w_refs, wb_refs):
            wb_ref[...] = w_ref[...].astype(BF16)

    dots = [jnp.dot(a_ref[...].astype(BF16), wb_ref[...], preferred_element_type=F32)
            for a_ref, wb_ref in zip(a_refs, wb_refs)]
    o_ref[...] = epilogue(dots, [e_ref[...] for e_ref in e_refs]).astype(o_ref.dtype)


def fused_mm(pairs, extras, epilogue, n_out, out_dtype, tm, tn, name):
    m = pairs[0][0].shape[0]
    assert m % tm == 0 and n_out % tn == 0
    in_specs, args, scratch = [], [], []
    for a, _, _ in pairs:
        k = a.shape[1]
        in_specs.append(pl.BlockSpec((tm, k), lambda j, i: (i, 0)))
        args.append(a)
    for a, w, off in pairs:
        k = a.shape[1]
        assert off % tn == 0 and w.shape[0] == k
        in_specs.append(pl.BlockSpec((k, tn), functools.partial(lambda j, i, o: (0, j + o), o=off // tn)))
        args.append(w)
        scratch.append(pltpu.VMEM((k, tn), BF16))
    for e in extras:
        in_specs.append(pl.BlockSpec((tm, tn), lambda j, i: (i, j)))
        args.append(e)
    kern = functools.partial(_fused_mm_kernel, n_pairs=len(pairs), n_extra=len(extras), epilogue=epilogue)
    return pl.pallas_call(
        kern,
        grid=(n_out // tn, m // tm),
        in_specs=in_specs,
        out_specs=pl.BlockSpec((tm, tn), lambda j, i: (i, j)),
        out_shape=jax.ShapeDtypeStruct((m, n_out), out_dtype),
        scratch_shapes=scratch,
        compiler_params=_cp(("arbitrary", "arbitrary")),
        name=name,
    )(*args)


def _swa_prompt_kernel(sink_ref, q_ref, kp_ref, kc_ref, vp_ref, vc_ref, o_ref):
    j = pl.program_id(1)
    w = WINDOW
    qi = lax.broadcasted_iota(I32, (w, 2 * w), 0)
    km = lax.broadcasted_iota(I32, (w, 2 * w), 1)
    band = (km >= qi) & (km <= w + qi) & ((j > 0) | (km >= w))
    mask4 = jnp.concatenate([band] * GROUP, axis=0)
    lane = lax.broadcasted_iota(I32, (1, LANES), 1)
    lo = lane < HEAD_DIM
    scale = HEAD_DIM ** -0.5
    for h in range(N_KV_HEADS):
        pair = (h // 2) * LANES
        keep = lo if h % 2 == 0 else ~lo
        k2 = jnp.concatenate([kp_ref[:, pair:pair + LANES], kc_ref[:, pair:pair + LANES]], axis=0)
        v2 = jnp.concatenate([vp_ref[:, pair:pair + LANES], vc_ref[:, pair:pair + LANES]], axis=0)
        k2 = jnp.where(keep, k2, 0.0)
        v2 = jnp.where(keep, v2, 0.0)
        k2 = k2 + pltpu.roll(k2, HEAD_DIM, axis=1)
        v2 = v2 + pltpu.roll(v2, HEAD_DIM, axis=1)
        qs, sink_cols = [], []
        for g in range(GROUP):
            n = h * GROUP + g
            qp = q_ref[:, (n // 2) * LANES:(n // 2 + 1) * LANES]
            qs.append(jnp.where(lo if n % 2 == 0 else ~lo, qp, 0.0))
            sink_cols.append(jnp.full((w, 1), sink_ref[n], F32))
        q4 = jnp.concatenate(qs, axis=0)
        sink = jnp.concatenate(sink_cols, axis=0)
        s = _bdot_nt(q4, k2) * scale
        s = jnp.where(mask4, s, NEG_BIG)
        m = jnp.maximum(jnp.max(s, axis=-1, keepdims=True), sink)
        p = jnp.exp(s - m)
        den = jnp.sum(p, axis=-1, keepdims=True) + jnp.exp(sink - m)
        o4 = _bdot(p, v2) / den
        for g in range(0, GROUP, 2):
            n = h * GROUP + g
            pair_out = jnp.where(lo, o4[g * w:(g + 1) * w], o4[(g + 1) * w:(g + 2) * w])
            o_ref[:, (n // 2) * LANES:(n // 2 + 1) * LANES] = pair_out.astype(o_ref.dtype)


def swa_prompt(proj, sinks, batch, seq):
    nb = seq // WINDOW
    row = lambda b, j: b * nb + j
    return pl.pallas_call(
        _swa_prompt_kernel,
        grid=(batch, nb),
        in_specs=[
            pl.BlockSpec(memory_space=pltpu.SMEM),
            pl.BlockSpec((WINDOW, ATT_Q), lambda b, j: (row(b, j), 0)),
            pl.BlockSpec((WINDOW, ATT_KV), lambda b, j: (row(b, jnp.maximum(j - 1, 0)), ATT_Q // ATT_KV)),
            pl.BlockSpec((WINDOW, ATT_KV), lambda b, j: (row(b, j), ATT_Q // ATT_KV)),
            pl.BlockSpec((WINDOW, ATT_KV), lambda b, j: (row(b, jnp.maximum(j - 1, 0)), ATT_Q // ATT_KV + 1)),
            pl.BlockSpec((WINDOW, ATT_KV), lambda b, j: (row(b, j), ATT_Q // ATT_KV + 1)),
        ],
        out_specs=pl.BlockSpec((WINDOW, ATT_Q), lambda b, j: (row(b, j), 0)),
        out_shape=jax.ShapeDtypeStruct((batch * seq, ATT_Q), BF16),
        compiler_params=_cp(("parallel", "arbitrary")),
        name="swa_prompt",
    )(sinks, proj, proj, proj, proj, proj)


def _swa_sample_kernel(sink_ref, q_ref, kn_ref, vn_ref, ck_ref, cv_ref, o_ref, *, t_new, w_cache):
    bg = q_ref.shape[0]
    nb = bg * N_KV_HEADS
    rows = GROUP * t_new
    q = q_ref[...].reshape(nb, rows, HEAD_DIM)
    kn = kn_ref[...].reshape(nb, t_new, HEAD_DIM)
    vn = vn_ref[...].reshape(nb, t_new, HEAD_DIM)
    ck = ck_ref[...].reshape(nb, w_cache, HEAD_DIM)
    cv = cv_ref[...].reshape(nb, w_cache, HEAD_DIM)
    scale = HEAD_DIM ** -0.5
    bdims = (((2,), (2,)), ((0,), (0,)))
    s_c = lax.dot_general(q.astype(BF16), ck.astype(BF16), bdims, preferred_element_type=F32) * scale
    s_n = lax.dot_general(q.astype(BF16), kn.astype(BF16), bdims, preferred_element_type=F32) * scale
    qi_c = lax.broadcasted_iota(I32, (nb, rows, w_cache), 1) % t_new
    kc = lax.broadcasted_iota(I32, (nb, rows, w_cache), 2)
    s_c = jnp.where(kc >= w_cache + qi_c - WINDOW, s_c, NEG_BIG)
    qi_n = lax.broadcasted_iota(I32, (nb, rows, t_new), 1) % t_new
    kk = lax.broadcasted_iota(I32, (nb, rows, t_new), 2)
    s_n = jnp.where(kk <= qi_n, s_n, NEG_BIG)
    bi = lax.broadcasted_iota(I32, (nb, rows, 1), 0) % N_KV_HEADS
    gi = lax.broadcasted_iota(I32, (nb, rows, 1), 1) // t_new
    head = bi * GROUP + gi
    sink = jnp.zeros((nb, rows, 1), F32)
    for n in range(N_Q_HEADS):
        sink = jnp.where(head == n, sink_ref[n], sink)
    m = jnp.maximum(jnp.maximum(jnp.max(s_c, axis=-1, keepdims=True), jnp.max(s_n, axis=-1, keepdims=True)), sink)
    p_c = jnp.exp(s_c - m)
    p_n = jnp.exp(s_n - m)
    den = jnp.sum(p_c, axis=-1, keepdims=True) + jnp.sum(p_n, axis=-1, keepdims=True) + jnp.exp(sink - m)
    pv = (((2,), (1,)), ((0,), (0,)))
    o = (lax.dot_general(p_c.astype(BF16), cv.astype(BF16), pv, preferred_element_type=F32)
         + lax.dot_general(p_n.astype(BF16), vn.astype(BF16), pv, preferred_element_type=F32)) / den
    o_ref[...] = o.reshape(bg, N_KV_HEADS, rows, HEAD_DIM).astype(o_ref.dtype)


def swa_sample(q, kn, vn, ck, cv, sinks, bg=8):
    b, _, rows, _ = q.shape
    t_new, w_cache = kn.shape[2], ck.shape[2]
    spec = lambda r: pl.BlockSpec((bg, N_KV_HEADS, r, HEAD_DIM), lambda i: (i, 0, 0, 0))
    return pl.pallas_call(
        functools.partial(_swa_sample_kernel, t_new=t_new, w_cache=w_cache),
        grid=(b // bg,),
        in_specs=[pl.BlockSpec(memory_space=pltpu.SMEM), spec(rows), spec(t_new), spec(t_new),
                  spec(w_cache), spec(w_cache)],
        out_specs=spec(rows),
        out_shape=jax.ShapeDtypeStruct(q.shape, F32),
        compiler_params=_cp(("parallel",)),
        name="swa_sample",
    )(sinks, q, kn, vn, ck, cv)


def _unit_lower_inverse(a):
    c = a.shape[0]
    ri = lax.broadcasted_iota(I32, (c, c), 0)
    ci = lax.broadcasted_iota(I32, (c, c), 1)
    x = -a
    t = jnp.where(ri == ci, 1.0, 0.0) + x
    p = x
    span = 2
    while span < c:
        p = _bdot(p, p)
        t = t + _bdot(t, p)
        span *= 2
    return t


def _gdn_chunk(qn, kn, v, beta, g, s):
    c = qn.shape[0]
    ri = lax.broadcasted_iota(I32, (c, c), 0)
    ci = lax.broadcasted_iota(I32, (c, c), 1)
    incl = ri >= ci
    strict = ri > ci
    gb = jnp.broadcast_to(g, (c, LANES))
    gcum = _fdot(jnp.where(incl, 1.0, 0.0), gb)
    gc = gcum[:, :c]
    g_row = _fdot(jnp.ones((c, c), F32), jnp.where(ri == ci, gc, 0.0))
    decay = jnp.where(incl, jnp.exp(jnp.where(incl, gc - g_row, 0.0)), 0.0)
    eg = jnp.exp(gcum)
    kb = kn * beta
    vb = v * beta
    ak = _bdot_nt(jnp.concatenate([kb, qn], axis=0), kn)
    a_mat = jnp.where(strict, ak[:c] * decay, 0.0)
    qk = jnp.where(incl, ak[c:] * decay, 0.0)
    t_inv = _unit_lower_inverse(a_mat)
    uw = _bdot(t_inv, jnp.concatenate([vb, kb * eg], axis=1))
    ws = _bdot(jnp.concatenate([uw[:, DV:], qn * eg], axis=0), s)
    u = uw[:, :DV] - ws[:c]
    o = ws[c:] + _bdot(qk, u)
    glast = gcum[c - 1:c, :]
    kend = kn * jnp.exp(glast - gcum)
    s_new = s * jnp.exp(glast) + _bdot_tn(kend, u)
    return o, s_new


def _conv_silu(ext_ref, x_ref, cw_ref, tb):
    ext_ref[SUBLANES:tb + SUBLANES, :] = x_ref[...]
    y = None
    for jj in range(CONV_W):
        term = ext_ref[SUBLANES - (CONV_W - 1) + jj:SUBLANES - (CONV_W - 1) + jj + tb, :] * cw_ref[jj:jj + 1, :]
        y = term if y is None else y + term
    ext_ref[0:SUBLANES, :] = ext_ref[tb:tb + SUBLANES, :]
    return y * _sigmoid(y)


def _l2norm(x):
    return x * lax.rsqrt(jnp.sum(x * x, axis=-1, keepdims=True) + EPS)


def _gdn_prompt_kernel(alog_ref, dtb_ref, q_ref, k_ref, v_ref, z_ref, ba_ref, cwq_ref, cwk_ref, cwv_ref,
                       dnw_ref, o_ref, sout_ref, extq, extk, extv, s_scr, *, tb):
    hg = pl.program_id(1)
    t = pl.program_id(2)

    @pl.when(t == 0)
    def _():
        zero = jnp.zeros((SUBLANES, extq.shape[1]), F32)
        extq[0:SUBLANES, :] = zero
        extk[0:SUBLANES, :] = zero
        extv[0:SUBLANES, :] = zero
        s_scr[...] = jnp.zeros(s_scr.shape, F32)

    qc = _conv_silu(extq, q_ref, cwq_ref, tb)
    kc = _conv_silu(extk, k_ref, cwk_ref, tb)
    vc = _conv_silu(extv, v_ref, cwv_ref, tb)
    ba = ba_ref[...]
    lane = lax.broadcasted_iota(I32, (tb, LANES), 1)
    dnw = dnw_ref[...]
    for hh in range(GDN_HG):
        h = hg * GDN_HG + hh
        b_col = jnp.sum(jnp.where(lane == h, ba, 0.0), axis=-1, keepdims=True)
        a_col = jnp.sum(jnp.where(lane == N_DN_HEADS + h, ba, 0.0), axis=-1, keepdims=True)
        beta = _sigmoid(b_col)
        g = -jnp.exp(jnp.full((tb, 1), alog_ref[h], F32)) * _softplus(a_col + dtb_ref[h])
        sl = slice(hh * DK, (hh + 1) * DK)
        qn = _l2norm(qc[:, sl]) * (DK ** -0.5)
        kn = _l2norm(kc[:, sl])
        vh = vc[:, sl]
        s = s_scr[hh]
        outs = []
        for c0 in range(0, tb, GDN_CHUNK):
            rs = slice(c0, c0 + GDN_CHUNK)
            o, s = _gdn_chunk(qn[rs], kn[rs], vh[rs], beta[rs], g[rs], s)
            outs.append(o)
        s_scr[hh] = s
        o = jnp.concatenate(outs, axis=0)
        o = o * lax.rsqrt(jnp.mean(o * o, axis=-1, keepdims=True) + EPS) * dnw
        zh = z_ref[:, sl]
        o_ref[:, sl] = (o * (zh * _sigmoid(zh))).astype(o_ref.dtype)

    @pl.when(t == pl.num_programs(2) - 1)
    def _():
        sout_ref[0] = s_scr[...]


def gdn_prompt(proj, ba, conv_w, a_log, dt_bias, dn_norm, batch, seq):
    tb = GDN_TBLOCK
    nt = seq // tb
    wblk = GDN_HG * DK
    q0, k0, v0, z0 = (COL_QKV_D // wblk, (COL_QKV_D + N_DN_HEADS * DK) // wblk,
                      (COL_QKV_D + 2 * N_DN_HEADS * DK) // wblk, COL_Z // wblk)
    nhg = N_DN_HEADS // GDN_HG
    row = lambda b, t: b * nt + t
    col_spec = lambda c0: pl.BlockSpec((tb, wblk), lambda b, hg, t: (row(b, t), c0 + hg))
    cw_spec = lambda c0: pl.BlockSpec((CONV_W, wblk), lambda b, hg, t: (0, c0 + hg))
    smem = pl.BlockSpec(memory_space=pltpu.SMEM)
    return pl.pallas_call(
        functools.partial(_gdn_prompt_kernel, tb=tb),
        grid=(batch, nhg, nt),
        in_specs=[smem, smem, col_spec(q0), col_spec(k0), col_spec(v0), col_spec(z0),
                  pl.BlockSpec((tb, LANES), lambda b, hg, t: (row(b, t), 0)),
                  cw_spec(0), cw_spec(nhg), cw_spec(2 * nhg),
                  pl.BlockSpec((1, DV), lambda b, hg, t: (0, 0))],
        out_specs=[pl.BlockSpec((tb, wblk), lambda b, hg, t: (row(b, t), hg)),
                   pl.BlockSpec((1, GDN_HG, DK, DV), lambda b, hg, t: (b, hg, 0, 0))],
        out_shape=[jax.ShapeDtypeStruct((batch * seq, DN_V), BF16),
                   jax.ShapeDtypeStruct((batch, N_DN_HEADS, DK, DV), F32)],
        scratch_shapes=[pltpu.VMEM((tb + SUBLANES, wblk), F32)] * 3 + [pltpu.VMEM((GDN_HG, DK, DV), F32)],
        compiler_params=_cp(("parallel", "parallel", "arbitrary")),
        name="gdn_prompt",
    )(a_log, dt_bias, proj, proj, proj, proj, ba, conv_w, conv_w, conv_w, dn_norm.reshape(1, DV))


def _gdn_sample_kernel(xp_ref, z_ref, ba_ref, cw_ref, alog_ref, dtb_ref, dnw_ref, s0_ref, o_ref, s_ref, *, t_new):
    bg = xp_ref.shape[0]
    rows = xp_ref.shape[1]
    ri = lax.broadcasted_iota(I32, (rows, LANES), 0)
    eye = jnp.where(lax.broadcasted_iota(I32, (LANES, LANES), 0) == lax.broadcasted_iota(I32, (LANES, LANES), 1),
                    1.0, 0.0)
    dnw = dnw_ref[...]

    def per_batch(b, carry):
        xp = xp_ref[b]
        y = None
        for jj in range(CONV_W):
            sh = xp if jj == 0 else pltpu.roll(xp, rows - jj, axis=0)
            term = sh * cw_ref[jj:jj + 1, :]
            y = term if y is None else y + term
        y = y * _sigmoid(y)
        ba = ba_ref[b]
        beta_all = _sigmoid(ba)
        g_all = -jnp.exp(alog_ref[...]) * _softplus(ba + dtb_ref[...])
        z = z_ref[b]
        for h in range(N_DN_HEADS):
            qn = _l2norm(y[:, h * DK:(h + 1) * DK]) * (DK ** -0.5)
            kn = _l2norm(y[:, (N_DN_HEADS + h) * DK:(N_DN_HEADS + h + 1) * DK])
            vh = y[:, (2 * N_DN_HEADS + h) * DK:(2 * N_DN_HEADS + h + 1) * DK]
            kq = jnp.where(ri < t_new, kn, pltpu.roll(qn, t_new, axis=0))
            kq_t = lax.dot_general(eye, kq, (((1,), (1,)), ((), ())), preferred_element_type=F32,
                                   precision=lax.Precision.HIGHEST)
            s = s0_ref[b, h]
            o_rows = jnp.zeros((rows, DV), F32)
            for tt in range(t_new):
                kcol = kq_t[:, tt:tt + 1]
                qcol = kq_t[:, t_new + tt:t_new + tt + 1]
                beta = beta_all[tt:tt + 1, h:h + 1]
                eg = jnp.exp(g_all[tt:tt + 1, N_DN_HEADS + h:N_DN_HEADS + h + 1])
                s = s * eg
                ks = jnp.sum(s * kcol, axis=0, keepdims=True)
                delta = beta * (vh[tt:tt + 1, :] - ks)
                s = s + kcol * delta
                o_t = jnp.sum(s * qcol, axis=0, keepdims=True)
                o_rows = jnp.where(ri == tt, o_t, o_rows)
            s_ref[b, h] = s
            o = o_rows * lax.rsqrt(jnp.mean(o_rows * o_rows, axis=-1, keepdims=True) + EPS) * dnw
            zh = z[:, h * DV:(h + 1) * DV]
            o_ref[b, :, h * DV:(h + 1) * DV] = (o * (zh * _sigmoid(zh))).astype(o_ref.dtype)
        return carry

    lax.fori_loop(0, bg, per_batch, 0)


def gdn_sample(xp, z, ba, conv_w, a_log, dt_bias, dn_norm, s0, t_new, bg=4):
    b = xp.shape[0]
    rows = xp.shape[1]
    pad16 = lambda v: jnp.zeros((1, LANES), F32).at[0, N_DN_HEADS:2 * N_DN_HEADS].set(v)
    full = lambda shape: pl.BlockSpec(shape, lambda i: (0,) * len(shape))
    return pl.pallas_call(
        functools.partial(_gdn_sample_kernel, t_new=t_new),
        grid=(b // bg,),
        in_specs=[pl.BlockSpec((bg, rows, CONV_DIM), lambda i: (i, 0, 0)),
                  pl.BlockSpec((bg, rows, DN_V), lambda i: (i, 0, 0)),
                  pl.BlockSpec((bg, rows, LANES), lambda i: (i, 0, 0)),
                  full((CONV_W, CONV_DIM)), full((1, LANES)), full((1, LANES)), full((1, DV)),
                  pl.BlockSpec((bg, N_DN_HEADS, DK, DV), lambda i: (i, 0, 0, 0))],
        out_specs=[pl.BlockSpec((bg, rows, DN_V), lambda i: (i, 0, 0)),
                   pl.BlockSpec((bg, N_DN_HEADS, DK, DV), lambda i: (i, 0, 0, 0))],
        out_shape=[jax.ShapeDtypeStruct((b, rows, DN_V), BF16),
                   jax.ShapeDtypeStruct((b, N_DN_HEADS, DK, DV), F32)],
        compiler_params=_cp(("parallel",)),
        name="gdn_sample",
    )(xp, z, ba, conv_w, pad16(a_log), pad16(dt_bias), dn_norm.reshape(1, DV), s0)


def _router_kernel(x_ref, g_ref, wr_ref, br_ref, mi_ref, mf_ref, cnt_ref, run_ref):
    i = pl.program_id(0)
    tm = x_ref.shape[0]

    @pl.when(i == 0)
    def _():
        run_ref[...] = jnp.zeros(run_ref.shape, F32)

    x = x_ref[...]
    xn = x * lax.rsqrt(jnp.mean(x * x, axis=-1, keepdims=True) + EPS) * g_ref[...]
    lane = lax.broadcasted_iota(I32, (tm, LANES), 1)
    lane_f = lane.astype(F32)
    logits = _fdot(xn, wr_ref[...]) + br_ref[...]
    work = jnp.where(lane < N_EXPERTS, logits, -jnp.inf)
    vals, idxs, hots = [], [], []
    for _ in range(TOP_K):
        mx = jnp.max(work, axis=-1, keepdims=True)
        idx_f = jnp.min(jnp.where(work == mx, lane_f, float(LANES)), axis=-1, keepdims=True)
        idx = idx_f.astype(I32)
        hot = lane == idx
        work = jnp.where(hot, -jnp.inf, work)
        vals.append(mx)
        idxs.append(idx)
        hots.append(hot)
    exps = [jnp.exp(v - vals[0]) for v in vals]
    den = exps[0] + exps[1] + exps[2] + exps[3]
    member = jnp.where(hots[0] | hots[1] | hots[2] | hots[3], 1.0, 0.0)
    ri = lax.broadcasted_iota(I32, (tm, tm), 0)
    ci = lax.broadcasted_iota(I32, (tm, tm), 1)
    before = _bdot(jnp.where(ri > ci, 1.0, 0.0), member) + run_ref[0:1, :]
    run_ref[...] = run_ref[...] + jnp.sum(member, axis=0, keepdims=True)
    mi = jnp.zeros((tm, LANES), I32)
    mf = jnp.zeros((tm, LANES), F32)
    for k in range(TOP_K):
        rank = jnp.sum(jnp.where(hots[k], before, 0.0), axis=-1, keepdims=True).astype(I32)
        mi = jnp.where(lane == k, idxs[k], mi)
        mi = jnp.where(lane == TOP_K + k, rank, mi)
        mf = jnp.where(lane == k, exps[k] / den, mf)
    mi_ref[...] = mi
    mf_ref[...] = mf
    cnt_ref[...] = run_ref[...]


def router(x, gamma, w_router, b_router):
    m, d = x.shape
    tm = TOK_TILE
    wr = jnp.zeros((d, LANES), F32).at[:, :N_EXPERTS].set(w_router)
    br = jnp.zeros((1, LANES), F32).at[0, :N_EXPERTS].set(b_router)
    return pl.pallas_call(
        _router_kernel,
        grid=(m // tm,),
        in_specs=[pl.BlockSpec((tm, d), lambda i: (i, 0)), pl.BlockSpec((1, d), lambda i: (0, 0)),
                  pl.BlockSpec((d, LANES), lambda i: (0, 0)), pl.BlockSpec((1, LANES), lambda i: (0, 0))],
        out_specs=[pl.BlockSpec((tm, LANES), lambda i: (i, 0)), pl.BlockSpec((tm, LANES), lambda i: (i, 0)),
                   pl.BlockSpec((SUBLANES, LANES), lambda i: (0, 0))],
        out_shape=[jax.ShapeDtypeStruct((m, LANES), I32), jax.ShapeDtypeStruct((m, LANES), F32),
                   jax.ShapeDtypeStruct((SUBLANES, LANES), F32)],
        scratch_shapes=[pltpu.VMEM((SUBLANES, LANES), F32)],
        compiler_params=_cp(("arbitrary",)),
        name="router",
    )(x, gamma.reshape(1, d), wr, br)


def _zero_kernel(sub_ref, o_ref):
    del sub_ref
    o_ref[...] = jnp.zeros(o_ref.shape, o_ref.dtype)


def zero_partial_subblocks(sub_idx, n_rows, d):
    return pl.pallas_call(
        _zero_kernel,
        grid_spec=pltpu.PrefetchScalarGridSpec(
            num_scalar_prefetch=1, grid=(sub_idx.shape[0],), in_specs=[],
            out_specs=pl.BlockSpec((MOE_SUB, d), lambda e, sub: (sub[e], 0))),
        out_shape=jax.ShapeDtypeStruct((n_rows, d), F32),
        compiler_params=_cp(("arbitrary",)),
        name="moe_zero_tails",
    )(sub_idx)


def _row_copy(src_ref, src_row, dst_ref, dst_row, sem):
    return pltpu.make_async_copy(src_ref.at[pl.ds(src_row, 1), :], dst_ref.at[pl.ds(dst_row, 1), :], sem)


def _load_dest(dest_ref, dest_smem, sem):
    cp = pltpu.make_async_copy(dest_ref, dest_smem, sem)
    cp.start()
    cp.wait()


def _dest_of(dest_smem, r, k):
    per_row = LANES // TOP_K
    return dest_smem[r // per_row, (r % per_row) * TOP_K + k]


def _dispatch_kernel(dest_ref, x_ref, xs_in_ref, xs_ref, dest_smem, sem_idx, sem):
    del xs_in_ref
    tm = x_ref.shape[0]
    _load_dest(dest_ref, dest_smem, sem_idx)

    def issue(r, c):
        for k in range(TOP_K):
            _row_copy(x_ref, r, xs_ref, _dest_of(dest_smem, r, k), sem).start()
        return c

    lax.fori_loop(0, tm, issue, 0)

    def drain(r, c):
        for k in range(TOP_K):
            _row_copy(x_ref, 0, xs_ref, 0, sem).wait()
        return c

    lax.fori_loop(0, tm, drain, 0)


def dispatch(x, dest2d, xs):
    m, d = x.shape
    tm = TOK_TILE
    drows = tm * TOP_K // LANES
    return pl.pallas_call(
        _dispatch_kernel,
        grid=(m // tm,),
        in_specs=[pl.BlockSpec((drows, LANES), lambda i: (i, 0)),
                  pl.BlockSpec((tm, d), lambda i: (i, 0)),
                  pl.BlockSpec(memory_space=pl.ANY)],
        out_specs=pl.BlockSpec(memory_space=pl.ANY),
        out_shape=jax.ShapeDtypeStruct(xs.shape, xs.dtype),
        scratch_shapes=[pltpu.SMEM((drows, LANES), I32), pltpu.SemaphoreType.DMA, pltpu.SemaphoreType.DMA],
        input_output_aliases={2: 0},
        compiler_params=_cp(("arbitrary",)),
        name="moe_dispatch",
    )(dest2d, x, xs)


def _expert_kernel(blk_ref, bexp_ref, nsub_ref, last_ref, x0_ref, x1_ref, x2_ref, x3_ref, g_ref, wg_ref, wu_ref,
                   bg_ref, bu_ref, wo_ref, bo_ref, o_ref, xn_ref):
    del blk_ref, bexp_ref, last_ref
    b = pl.program_id(0)
    f = pl.program_id(1)
    n = nsub_ref[b]
    x_refs = (x0_ref, x1_ref, x2_ref, x3_ref)

    @pl.when(f == 0)
    def _():
        for s, x_ref in enumerate(x_refs):
            rows = slice(s * MOE_SUB, (s + 1) * MOE_SUB)

            @pl.when(s < n)
            def _():
                x = x_ref[...]
                xn = x * lax.rsqrt(jnp.mean(x * x, axis=-1, keepdims=True) + EPS) * g_ref[...]
                xn_ref[rows, :] = xn.astype(BF16)
                o_ref[rows, :] = jnp.broadcast_to(bo_ref[...], (MOE_SUB, o_ref.shape[1]))

            @pl.when((n > 0) & (s >= n))
            def _():
                o_ref[rows, :] = jnp.zeros((MOE_SUB, o_ref.shape[1]), F32)

    wg = wg_ref[...].astype(BF16)
    wu = wu_ref[...].astype(BF16)
    wo = wo_ref[...].astype(BF16)

    def sub(s, c):
        rows = pl.ds(pl.multiple_of(s * MOE_SUB, MOE_SUB), MOE_SUB)
        xs = xn_ref[rows, :]
        gate = jnp.dot(xs, wg, preferred_element_type=F32) + bg_ref[...]
        up = jnp.dot(xs, wu, preferred_element_type=F32) + bu_ref[...]
        gate = jnp.minimum(gate, SWIGLU_LIMIT)
        up = jnp.clip(up, -SWIGLU_LIMIT, SWIGLU_LIMIT)
        act = (up + 1.0) * gate * _sigmoid(SWIGLU_ALPHA * gate)
        o_ref[rows, :] += jnp.dot(act.astype(BF16), wo, preferred_element_type=F32)
        return c

    lax.fori_loop(0, n, sub, 0)


def experts(xs, gamma, w_in, b_in, w_out, b_out, blk, bexp, nsub, last_sub):
    n_rows, d = xs.shape
    nb = blk.shape[0]
    nf = D_FF // MOE_TF
    per = MOE_ROWS // MOE_SUB
    assert per == 4

    def x_spec(s):
        return pl.BlockSpec((MOE_SUB, d),
                            lambda b, f, blk, bexp, nsub, last: (jnp.minimum(per * blk[b] + s, last[b]), 0))

    wsel = lambda b, f, blk, bexp, nsub, last: bexp[b]
    in_specs = [x_spec(0), x_spec(1), x_spec(2), x_spec(3),
                pl.BlockSpec((1, d), lambda b, f, *_: (0, 0)),
                pl.BlockSpec((None, d, MOE_TF), lambda b, f, *p: (wsel(b, f, *p), 0, f)),
                pl.BlockSpec((None, d, MOE_TF), lambda b, f, *p: (wsel(b, f, *p), 0, nf + f)),
                pl.BlockSpec((None, 1, MOE_TF), lambda b, f, *p: (wsel(b, f, *p), 0, f)),
                pl.BlockSpec((None, 1, MOE_TF), lambda b, f, *p: (wsel(b, f, *p), 0, nf + f)),
                pl.BlockSpec((None, MOE_TF, d), lambda b, f, *p: (wsel(b, f, *p), f, 0)),
                pl.BlockSpec((None, 1, d), lambda b, f, *p: (wsel(b, f, *p), 0, 0))]
    return pl.pallas_call(
        _expert_kernel,
        grid_spec=pltpu.PrefetchScalarGridSpec(
            num_scalar_prefetch=4, grid=(nb, nf), in_specs=in_specs,
            out_specs=pl.BlockSpec((MOE_ROWS, d), lambda b, f, blk, bexp, nsub, last: (blk[b], 0)),
            scratch_shapes=[pltpu.VMEM((MOE_ROWS, d), BF16)]),
        out_shape=jax.ShapeDtypeStruct((n_rows, d), F32),
        compiler_params=_cp(("arbitrary", "arbitrary")),
        name="moe_experts",
    )(blk, bexp, nsub, last_sub, xs, xs, xs, xs, gamma.reshape(1, d), w_in, w_in,
      b_in.reshape(N_EXPERTS, 1, 2 * D_FF), b_in.reshape(N_EXPERTS, 1, 2 * D_FF), w_out,
      b_out.reshape(N_EXPERTS, 1, d))


def _combine_kernel(dest_ref, x_ref, mf_ref, ys_ref, o_ref, buf, dest_smem, sem_idx, sem):
    tm = x_ref.shape[0]
    _load_dest(dest_ref, dest_smem, sem_idx)

    def issue(r, c):
        for k in range(TOP_K):
            _row_copy(ys_ref, _dest_of(dest_smem, r, k), buf.at[k], r, sem).start()
        return c

    lax.fori_loop(0, tm, issue, 0)

    def drain(r, c):
        for k in range(TOP_K):
            _row_copy(ys_ref, 0, buf.at[k], 0, sem).wait()
        return c

    lax.fori_loop(0, tm, drain, 0)
    gates = mf_ref[...]
    acc = x_ref[...]
    for k in range(TOP_K):
        acc = acc + gates[:, k:k + 1] * buf[k]
    o_ref[...] = acc


def combine(x, mf, dest2d, ys):
    m, d = x.shape
    tm = TOK_TILE
    drows = tm * TOP_K // LANES
    return pl.pallas_call(
        _combine_kernel,
        grid=(m // tm,),
        in_specs=[pl.BlockSpec((drows, LANES), lambda i: (i, 0)),
                  pl.BlockSpec((tm, d), lambda i: (i, 0)),
                  pl.BlockSpec((tm, LANES), lambda i: (i, 0)),
                  pl.BlockSpec(memory_space=pl.ANY)],
        out_specs=pl.BlockSpec((tm, d), lambda i: (i, 0)),
        out_shape=jax.ShapeDtypeStruct((m, d), F32),
        scratch_shapes=[pltpu.VMEM((TOP_K, tm, d), F32), pltpu.SMEM((drows, LANES), I32),
                        pltpu.SemaphoreType.DMA, pltpu.SemaphoreType.DMA],
        compiler_params=_cp(("arbitrary",)),
        name="moe_combine",
    )(dest2d, x, mf, ys)


def routed_experts_residual(x, gamma, w_router, b_router, w_exp_in, b_exp_in, w_exp_out, b_exp_out):
    m, d = x.shape
    mi, mf, cnt = router(x, gamma, w_router, b_router)
    idx = mi[:, :TOP_K]
    rank = mi[:, TOP_K:2 * TOP_K]
    counts = cnt[0, :N_EXPERTS].astype(I32)
    padded = (counts + MOE_ROWS - 1) // MOE_ROWS * MOE_ROWS
    pad_end = jnp.cumsum(padded)
    pad_start = pad_end - padded
    onehot = idx[:, :, None] == jnp.arange(N_EXPERTS, dtype=I32)[None, None, :]
    dest = jnp.sum(jnp.where(onehot, pad_start[None, None, :], 0), axis=-1) + rank
    dest2d = dest.reshape(m * TOP_K // LANES, LANES)
    n_blocks = (m * TOP_K + N_EXPERTS * (MOE_ROWS - 1) + MOE_ROWS - 1) // MOE_ROWS
    used = pad_end[-1] // MOE_ROWS
    bids = jnp.arange(n_blocks, dtype=I32)
    blk = jnp.minimum(bids, jnp.maximum(used - 1, 0))
    bexp = jnp.minimum(jnp.sum((pad_end[None, :] <= (blk * MOE_ROWS)[:, None]).astype(I32), axis=1), N_EXPERTS - 1)
    fill_end = pad_start + counts
    valid_rows = jnp.clip(fill_end[bexp] - blk * MOE_ROWS, 0, MOE_ROWS)
    nsub = jnp.where(bids < used, (valid_rows + MOE_SUB - 1) // MOE_SUB, 0).astype(I32)
    per = MOE_ROWS // MOE_SUB
    last_sub = (per * blk + jnp.maximum(nsub, 1) - 1).astype(I32)
    n_rows = n_blocks * MOE_ROWS
    tail_sub = jnp.minimum(fill_end // MOE_SUB, n_rows // MOE_SUB - 1).astype(I32)
    xs = zero_partial_subblocks(tail_sub, n_rows, d)
    xs = dispatch(x, dest2d, xs)
    ys = experts(xs, gamma, w_exp_in, b_exp_in, w_exp_out, b_exp_out, blk, bexp, nsub, last_sub)
    return combine(x, mf, dest2d, ys)


def _ep_plain(dots, extras):
    return dots[0]


def _ep_sigmoid(dots, extras):
    return _sigmoid(dots[0])


def _ep_merge(dots, extras):
    return extras[0] * dots[0] + extras[1] * dots[1]


def _ep_residual(dots, extras):
    return extras[0] + dots[0]


def _ep_ple(dots, extras):
    return extras[0] + _sigmoid(dots[0]) * dots[1]


def kernel(x_prompt, x_sample, cache_k, cache_v, state_conv, state_ssm, p_prompt, p_sample, w_in, sinks, conv_w, a_log, dt_bias, dn_norm, w_attn_branch, w_dn_branch, w_out, w_router, b_router, w_exp_in, b_exp_in, w_exp_out, b_exp_out, w_ple_proj, w_ple_gate, norm_mix, norm_moe, norm_ple, norm_final):
    depth = w_in.shape[0]
    bp, seq, d = x_prompt.shape
    bs, t_new, _ = x_sample.shape
    w_cache = cache_k.shape[2]
    mp, ms = bp * seq, bs * t_new
    x = jnp.concatenate([x_prompt.reshape(mp, d), x_sample.reshape(ms, d)], axis=0)
    p_all = jnp.concatenate([p_prompt.reshape(depth, mp, -1), p_sample.reshape(depth, ms, -1)], axis=1)
    m = mp + ms
    tm_big = 1536 if m % 1536 == 0 else 512
    tm_mid = 768 if m % 768 == 0 else 512
    outs = {k: [] for k in ("kp", "vp", "cp", "sp", "ks", "vs", "cs", "ss")}
    for i in range(depth):
        h0 = rms_rows(x, norm_mix[i], BF16)
        proj = fused_mm([(h0, w_in[i], 0)], [], _ep_plain, COL_BA, F32, tm_big, 512, "proj_main")
        w_ba = jnp.zeros((d, LANES), F32).at[:, :2 * N_DN_HEADS].set(w_in[i][:, COL_BA:COL_GATES])
        ba = fused_mm([(h0, w_ba, 0)], [], _ep_plain, LANES, F32, tm_big, LANES, "proj_ba")
        gates = fused_mm([(h0, w_in[i][:, COL_GATES:], 0)], [], _ep_sigmoid, 2 * d, F32, tm_big, 512, "proj_gates")

        o_a_p = swa_prompt(proj, sinks[i], bp, seq)
        ps = proj[mp:]
        q_s = ps[:, :ATT_Q].reshape(bs, t_new, N_KV_HEADS, GROUP, HEAD_DIM).transpose(0, 2, 3, 1, 4)
        q_s = q_s.reshape(bs, N_KV_HEADS, GROUP * t_new, HEAD_DIM)
        k_s = ps[:, ATT_Q:ATT_Q + ATT_KV].reshape(bs, t_new, N_KV_HEADS, HEAD_DIM)
        v_s = ps[:, ATT_Q + ATT_KV:COL_QKV_D].reshape(bs, t_new, N_KV_HEADS, HEAD_DIM)
        o_a_s = swa_sample(q_s, k_s.transpose(0, 2, 1, 3), v_s.transpose(0, 2, 1, 3),
                           cache_k[i].transpose(0, 2, 1, 3), cache_v[i].transpose(0, 2, 1, 3), sinks[i])
        o_a_s = o_a_s.reshape(bs, N_KV_HEADS, GROUP, t_new, HEAD_DIM).transpose(0, 3, 1, 2, 4).reshape(ms, ATT_Q)
        o_a = jnp.concatenate([o_a_p, o_a_s.astype(BF16)], axis=0)
        outs["kp"].append(lax.slice(proj, (0, ATT_Q), (mp, ATT_Q + ATT_KV)).reshape(bp, seq, N_KV_HEADS, HEAD_DIM)[:, -WINDOW:])
        outs["vp"].append(lax.slice(proj, (0, ATT_Q + ATT_KV), (mp, COL_QKV_D)).reshape(bp, seq, N_KV_HEADS, HEAD_DIM)[:, -WINDOW:])
        outs["ks"].append(jnp.concatenate([cache_k[i], k_s], axis=1)[:, -w_cache:])
        outs["vs"].append(jnp.concatenate([cache_v[i], v_s], axis=1)[:, -w_cache:])

        o_d_p, s_p = gdn_prompt(proj, ba, conv_w[i], a_log[i], dt_bias[i], dn_norm[i], bp, seq)
        qkv_s = ps[:, COL_QKV_D:COL_Z].reshape(bs, t_new, CONV_DIM)
        rows = SUBLANES
        pad_rows = rows - (CONV_W - 1) - t_new
        xp_s = jnp.concatenate([state_conv[i], qkv_s, jnp.zeros((bs, pad_rows, CONV_DIM), F32)], axis=1)
        tok_pad = lambda a: jnp.pad(a.reshape(bs, t_new, -1), ((0, 0), (0, rows - t_new), (0, 0)))
        o_d_s, s_s = gdn_sample(xp_s, tok_pad(ps[:, COL_Z:COL_BA]), tok_pad(ba[mp:]), conv_w[i], a_log[i],
                                dt_bias[i], dn_norm[i], state_ssm[i], t_new)
        o_d = jnp.concatenate([o_d_p, o_d_s[:, :t_new].reshape(ms, DN_V)], axis=0)
        qkv_p = lax.slice(proj, (0, COL_QKV_D), (mp, COL_Z)).reshape(bp, seq, CONV_DIM)
        outs["cp"].append(qkv_p[:, -(CONV_W - 1):])
        outs["cs"].append(jnp.concatenate([state_conv[i], qkv_s], axis=1)[:, -(CONV_W - 1):])
        outs["sp"].append(s_p)
        outs["ss"].append(s_s)

        merged = _merge_call(o_a, o_d, w_attn_branch[i], w_dn_branch[i], gates, tm_mid)
        x = fused_mm([(merged, w_out[i], 0)], [x], _ep_residual, d, F32, tm_mid, 512, "out_proj")

        x = routed_experts_residual(x, norm_moe[i], w_router[i], b_router[i], w_exp_in[i], b_exp_in[i],
                                    w_exp_out[i], b_exp_out[i])

        hp = rms_rows(x, norm_ple[i], BF16)
        x = fused_mm([(hp, w_ple_gate[i], 0), (p_all[i], w_ple_proj[i], 0)], [x], _ep_ple, d, F32, tm_mid, 512,
                     "ple")
    y = rms_rows(x, norm_final, F32)
    st = lambda k: jnp.stack(outs[k])
    return (y[:mp].reshape(bp, seq, d), y[mp:].reshape(bs, t_new, d), st("kp"), st("vp"), st("cp"), st("sp"),
            st("ks"), st("vs"), st("cs"), st("ss"))


def _merge_call(o_a, o_d, w_a, w_b, gates, tm):
    m = o_a.shape[0]
    d = w_a.shape[1]
    tn = 512
    nj = d // tn
    ka, kb = o_a.shape[1], o_d.shape[1]

    def body(a_ref, b_ref, wa_ref, wb_ref, ga_ref, gb_ref, o_ref, wab_ref, wbb_ref):
        @pl.when(pl.program_id(1) == 0)
        def _():
            wab_ref[...] = wa_ref[...].astype(BF16)
            wbb_ref[...] = wb_ref[...].astype(BF16)

        da = jnp.dot(a_ref[...], wab_ref[...], preferred_element_type=F32)
        db = jnp.dot(b_ref[...], wbb_ref[...], preferred_element_type=F32)
        o_ref[...] = (ga_ref[...] * da + gb_ref[...] * db).astype(o_ref.dtype)

    return pl.pallas_call(
        body,
        grid=(nj, m // tm),
        in_specs=[pl.BlockSpec((tm, ka), lambda j, i: (i, 0)), pl.BlockSpec((tm, kb), lambda j, i: (i, 0)),
                  pl.BlockSpec((ka, tn), lambda j, i: (0, j)), pl.BlockSpec((kb, tn), lambda j, i: (0, j)),
                  pl.BlockSpec((tm, tn), lambda j, i: (i, j)), pl.BlockSpec((tm, tn), lambda j, i: (i, nj + j))],
        out_specs=pl.BlockSpec((tm, tn), lambda j, i: (i, j)),
        out_shape=jax.ShapeDtypeStruct((m, d), BF16),
        scratch_shapes=[pltpu.VMEM((ka, tn), BF16), pltpu.VMEM((kb, tn), BF16)],
        compiler_params=_cp(("arbitrary", "arbitrary")),
        name="branch_merge",
    )(o_a, o_d, w_a, w_b, gates, gates)
```

```python
import functools

import jax
import jax.numpy as jnp
from jax import lax
from jax.experimental import pallas as pl
from jax.experimental.pallas import tpu as pltpu

F32 = jnp.float32
BF16 = jnp.bfloat16
I32 = jnp.int32

D_MODEL = 2048
N_Q_HEADS = 16
N_KV_HEADS = 4
GROUP = N_Q_HEADS // N_KV_HEADS
HEAD_DIM = 64
WINDOW = 128
N_DN_HEADS = 16
DK = 128
DV = 128
CONV_W = 4
CONV_DIM = N_DN_HEADS * (2 * DK + DV)
N_EXPERTS = 32
TOP_K = 4
D_FF = D_MODEL
SWIGLU_LIMIT = 7.0
SWIGLU_ALPHA = 1.702
EPS = 1e-6
ATT_Q = N_Q_HEADS * HEAD_DIM
ATT_KV = N_KV_HEADS * HEAD_DIM
DN_V = N_DN_HEADS * DV

COL_QKV_D = ATT_Q + 2 * ATT_KV
COL_Z = COL_QKV_D + CONV_DIM
COL_BA = COL_Z + DN_V
COL_GATES = COL_BA + 2 * N_DN_HEADS

LANES = 128
SUBLANES = 8
VMEM_LIMIT_BYTES = 56 * 2 ** 20

GDN_CHUNK = 64
GDN_TBLOCK = 256
GDN_HG = 4
GDN_SAMPLE_HEADS = 4
MOE_ROWS = 1024
MOE_SUB = 256
MOE_TF = 256
TOK_TILE = 256
NEG_BIG = -1e30


def _cp(sem, vmem=VMEM_LIMIT_BYTES):
    return pltpu.CompilerParams(dimension_semantics=sem, vmem_limit_bytes=vmem)


def _sigmoid(x):
    return 1.0 / (1.0 + jnp.exp(-x))


def _softplus(x):
    return jnp.maximum(x, 0.0) + jnp.log1p(jnp.exp(-jnp.abs(x)))


def _bdot(a, b):
    return jnp.dot(a.astype(BF16), b.astype(BF16), preferred_element_type=F32)


def _bdot_nt(a, b):
    return lax.dot_general(a.astype(BF16), b.astype(BF16), (((1,), (1,)), ((), ())),
                           preferred_element_type=F32)


def _bdot_tn(a, b):
    return lax.dot_general(a.astype(BF16), b.astype(BF16), (((0,), (0,)), ((), ())),
                           preferred_element_type=F32)


def _fdot(a, b):
    return jnp.dot(a, b, preferred_element_type=F32, precision=lax.Precision.HIGHEST)


def _rms_kernel(x_ref, g_ref, o_ref):
    x = x_ref[...]
    y = x * lax.rsqrt(jnp.mean(x * x, axis=-1, keepdims=True) + EPS)
    o_ref[...] = (y * g_ref[...]).astype(o_ref.dtype)


def rms_rows(x, gamma, out_dtype, tm=512):
    m, d = x.shape
    return pl.pallas_call(
        _rms_kernel,
        grid=(m // tm,),
        in_specs=[pl.BlockSpec((tm, d), lambda i: (i, 0)), pl.BlockSpec((1, d), lambda i: (0, 0))],
        out_specs=pl.BlockSpec((tm, d), lambda i: (i, 0)),
        out_shape=jax.ShapeDtypeStruct((m, d), out_dtype),
        compiler_params=_cp(("parallel",)),
        name="rms_rows",
    )(x, gamma.reshape(1, d))


def _fused_mm_kernel(*refs, n_pairs, n_extra, epilogue):
    a_refs = refs[:n_pairs]
    w_refs = refs[n_pairs:2 * n_pairs]
    e_refs = refs[2 * n_pairs:2 * n_pairs + n_extra]
    o_ref = refs[2 * n_pairs + n_extra]
    wb_refs = refs[2 * n_pairs + n_extra + 1:]

    @pl.when(pl.program_id(1) == 0)
    def _():
        for w_ref, wb_ref in zip(w_refs, wb_refs):
            wb_ref[...] = w_ref[...].astype(BF16)

    dots = [jnp.dot(a_ref[...].astype(BF16), wb_ref[...], preferred_element_type=F32)
            for a_ref, wb_ref in zip(a_refs, wb_refs)]
    o_ref[...] = epilogue(dots, [e_ref[...] for e_ref in e_refs]).astype(o_ref.dtype)


def fused_mm(pairs, extras, epilogue, n_out, out_dtype, tm, tn, name):
    m = pairs[0][0].shape[0]
    assert m % tm == 0 and n_out % tn == 0
    in_specs, args, scratch = [], [], []
    for a, _, _ in pairs:
        k = a.shape[1]
        in_specs.append(pl.BlockSpec((tm, k), lambda j, i: (i, 0)))
        args.append(a)
    for a, w, off in pairs:
        k = a.shape[1]
        assert off % tn == 0 and w.shape[0] == k
        in_specs.append(pl.BlockSpec((k, tn), functools.partial(lambda j, i, o: (0, j + o), o=off // tn)))
        args.append(w)
        scratch.append(pltpu.VMEM((k, tn), BF16))
    for e in extras:
        in_specs.append(pl.BlockSpec((tm, tn), lambda j, i: (i, j)))
        args.append(e)
    kern = functools.partial(_fused_mm_kernel, n_pairs=len(pairs), n_extra=len(extras), epilogue=epilogue)
    return pl.pallas_call(
        kern,
        grid=(n_out // tn, m // tm),
        in_specs=in_specs,
        out_specs=pl.BlockSpec((tm, tn), lambda j, i: (i, j)),
        out_shape=jax.ShapeDtypeStruct((m, n_out), out_dtype),
        scratch_shapes=scratch,
        compiler_params=_cp(("arbitrary", "arbitrary")),
        name=name,
    )(*args)


def _swa_prompt_kernel(sink_ref, q_ref, kp_ref, kc_ref, vp_ref, vc_ref, o_ref):
    j = pl.program_id(1)
    w = WINDOW
    qi = lax.broadcasted_iota(I32, (w, 2 * w), 0)
    km = lax.broadcasted_iota(I32, (w, 2 * w), 1)
    band = (km >= qi) & (km <= w + qi) & ((j > 0) | (km >= w))
    mask4 = jnp.concatenate([band] * GROUP, axis=0)
    lane = lax.broadcasted_iota(I32, (1, LANES), 1)
    lo = lane < HEAD_DIM
    scale = HEAD_DIM ** -0.5
    for h in range(N_KV_HEADS):
        pair = (h // 2) * LANES
        keep = lo if h % 2 == 0 else ~lo
        k2 = jnp.concatenate([kp_ref[:, pair:pair + LANES], kc_ref[:, pair:pair + LANES]], axis=0)
        v2 = jnp.concatenate([vp_ref[:, pair:pair + LANES], vc_ref[:, pair:pair + LANES]], axis=0)
        k2 = jnp.where(keep, k2, 0.0)
        v2 = jnp.where(keep, v2, 0.0)
        k2 = k2 + pltpu.roll(k2, HEAD_DIM, axis=1)
        v2 = v2 + pltpu.roll(v2, HEAD_DIM, axis=1)
        qs, sink_cols = [], []
        for g in range(GROUP):
            n = h * GROUP + g
            qp = q_ref[:, (n // 2) * LANES:(n // 2 + 1) * LANES]
            qs.append(jnp.where(lo if n % 2 == 0 else ~lo, qp, 0.0))
            sink_cols.append(jnp.full((w, 1), sink_ref[n], F32))
        q4 = jnp.concatenate(qs, axis=0)
        sink = jnp.concatenate(sink_cols, axis=0)
        s = _bdot_nt(q4, k2) * scale
        s = jnp.where(mask4, s, NEG_BIG)
        m = jnp.maximum(jnp.max(s, axis=-1, keepdims=True), sink)
        p = jnp.exp(s - m)
        den = jnp.sum(p, axis=-1, keepdims=True) + jnp.exp(sink - m)
        o4 = _bdot(p, v2) / den
        for g in range(0, GROUP, 2):
            n = h * GROUP + g
            pair_out = jnp.where(lo, o4[g * w:(g + 1) * w], o4[(g + 1) * w:(g + 2) * w])
            o_ref[:, (n // 2) * LANES:(n // 2 + 1) * LANES] = pair_out.astype(o_ref.dtype)


def swa_prompt(proj, sinks, batch, seq):
    nb = seq // WINDOW
    row = lambda b, j: b * nb + j
    return pl.pallas_call(
        _swa_prompt_kernel,
        grid=(batch, nb),
        in_specs=[
            pl.BlockSpec(memory_space=pltpu.SMEM),
            pl.BlockSpec((WINDOW, ATT_Q), lambda b, j: (row(b, j), 0)),
            pl.BlockSpec((WINDOW, ATT_KV), lambda b, j: (row(b, jnp.maximum(j - 1, 0)), ATT_Q // ATT_KV)),
            pl.BlockSpec((WINDOW, ATT_KV), lambda b, j: (row(b, j), ATT_Q // ATT_KV)),
            pl.BlockSpec((WINDOW, ATT_KV), lambda b, j: (row(b, jnp.maximum(j - 1, 0)), ATT_Q // ATT_KV + 1)),
            pl.BlockSpec((WINDOW, ATT_KV), lambda b, j: (row(b, j), ATT_Q // ATT_KV + 1)),
        ],
        out_specs=pl.BlockSpec((WINDOW, ATT_Q), lambda b, j: (row(b, j), 0)),
        out_shape=jax.ShapeDtypeStruct((batch * seq, ATT_Q), BF16),
        compiler_params=_cp(("parallel", "arbitrary")),
        name="swa_prompt",
    )(sinks, proj, proj, proj, proj, proj)


def _swa_sample_kernel(sink_ref, q_ref, kn_ref, vn_ref, ck_ref, cv_ref, o_ref, *, t_new, w_cache):
    bg = q_ref.shape[0]
    nb = bg * N_KV_HEADS
    rows = GROUP * t_new
    q = q_ref[...].reshape(nb, rows, HEAD_DIM)
    kn = kn_ref[...].reshape(nb, t_new, HEAD_DIM)
    vn = vn_ref[...].reshape(nb, t_new, HEAD_DIM)
    ck = ck_ref[...].reshape(nb, w_cache, HEAD_DIM)
    cv = cv_ref[...].reshape(nb, w_cache, HEAD_DIM)
    scale = HEAD_DIM ** -0.5
    bdims = (((2,), (2,)), ((0,), (0,)))
    s_c = lax.dot_general(q.astype(BF16), ck.astype(BF16), bdims, preferred_element_type=F32) * scale
    s_n = lax.dot_general(q.astype(BF16), kn.astype(BF16), bdims, preferred_element_type=F32) * scale
    qi_c = lax.broadcasted_iota(I32, (nb, rows, w_cache), 1) % t_new
    kc = lax.broadcasted_iota(I32, (nb, rows, w_cache), 2)
    s_c = jnp.where(kc >= w_cache + qi_c - WINDOW, s_c, NEG_BIG)
    qi_n = lax.broadcasted_iota(I32, (nb, rows, t_new), 1) % t_new
    kk = lax.broadcasted_iota(I32, (nb, rows, t_new), 2)
    s_n = jnp.where(kk <= qi_n, s_n, NEG_BIG)
    bi = lax.broadcasted_iota(I32, (nb, rows, 1), 0) % N_KV_HEADS
    gi = lax.broadcasted_iota(I32, (nb, rows, 1), 1) // t_new
    head = bi * GROUP + gi
    sink = jnp.zeros((nb, rows, 1), F32)
    for n in range(N_Q_HEADS):
        sink = jnp.where(head == n, sink_ref[n], sink)
    m = jnp.maximum(jnp.maximum(jnp.max(s_c, axis=-1, keepdims=True), jnp.max(s_n, axis=-1, keepdims=True)), sink)
    p_c = jnp.exp(s_c - m)
    p_n = jnp.exp(s_n - m)
    den = jnp.sum(p_c, axis=-1, keepdims=True) + jnp.sum(p_n, axis=-1, keepdims=True) + jnp.exp(sink - m)
    pv = (((2,), (1,)), ((0,), (0,)))
    o = (lax.dot_general(p_c.astype(BF16), cv.astype(BF16), pv, preferred_element_type=F32)
         + lax.dot_general(p_n.astype(BF16), vn.astype(BF16), pv, preferred_element_type=F32)) / den
    o_ref[...] = o.reshape(bg, N_KV_HEADS, rows, HEAD_DIM).astype(o_ref.dtype)


def swa_sample(q, kn, vn, ck, cv, sinks, bg=8):
    b, _, rows, _ = q.shape
    t_new, w_cache = kn.shape[2], ck.shape[2]
    spec = lambda r: pl.BlockSpec((bg, N_KV_HEADS, r, HEAD_DIM), lambda i: (i, 0, 0, 0))
    return pl.pallas_call(
        functools.partial(_swa_sample_kernel, t_new=t_new, w_cache=w_cache),
        grid=(b // bg,),
        in_specs=[pl.BlockSpec(memory_space=pltpu.SMEM), spec(rows), spec(t_new), spec(t_new),
                  spec(w_cache), spec(w_cache)],
        out_specs=spec(rows),
        out_shape=jax.ShapeDtypeStruct(q.shape, F32),
        compiler_params=_cp(("parallel",)),
        name="swa_sample",
    )(sinks, q, kn, vn, ck, cv)


def _unit_lower_inverses(a_list):
    c = a_list[0].shape[0]
    ri = lax.broadcasted_iota(I32, (c, c), 0)
    ci = lax.broadcasted_iota(I32, (c, c), 1)
    eye = jnp.where(ri == ci, 1.0, 0.0)
    ps = [-a for a in a_list]
    ts = [eye + p for p in ps]
    span = 2
    while span < c:
        ps = [_bdot(p, p) for p in ps]
        ts = [t + _bdot(t, p) for t, p in zip(ts, ps)]
        span *= 2
    return ts


def _conv_silu(ext_ref, x_ref, cw_ref, tb):
    ext_ref[SUBLANES:tb + SUBLANES, :] = x_ref[...]
    y = None
    for jj in range(CONV_W):
        term = ext_ref[SUBLANES - (CONV_W - 1) + jj:SUBLANES - (CONV_W - 1) + jj + tb, :] * cw_ref[jj:jj + 1, :]
        y = term if y is None else y + term
    ext_ref[0:SUBLANES, :] = ext_ref[tb:tb + SUBLANES, :]
    return y * _sigmoid(y)


def _l2norm(x):
    return x * lax.rsqrt(jnp.sum(x * x, axis=-1, keepdims=True) + EPS)


def _pad_a_lanes(v):
    return jnp.zeros((1, LANES), F32).at[0, N_DN_HEADS:2 * N_DN_HEADS].set(v)


def _gdn_prompt_kernel(q_ref, k_ref, v_ref, z_ref, ba_ref, cwq_ref, cwk_ref, cwv_ref, alog_ref, dtb_ref,
                       dnw_ref, o_ref, sout_ref, extq, extk, extv, s_scr, *, tb):
    hg = pl.program_id(1)
    t = pl.program_id(2)
    c = GDN_CHUNK
    nch = tb // c

    @pl.when(t == 0)
    def _():
        zero = jnp.zeros((SUBLANES, extq.shape[1]), F32)
        extq[0:SUBLANES, :] = zero
        extk[0:SUBLANES, :] = zero
        extv[0:SUBLANES, :] = zero
        s_scr[...] = jnp.zeros(s_scr.shape, F32)

    qc = _conv_silu(extq, q_ref, cwq_ref, tb)
    kc = _conv_silu(extk, k_ref, cwk_ref, tb)
    vc = _conv_silu(extv, v_ref, cwv_ref, tb)
    ba = ba_ref[...]
    lane = lax.broadcasted_iota(I32, (tb, LANES), 1)
    beta_all = _sigmoid(ba)
    g_all = -jnp.exp(alog_ref[...]) * _softplus(ba + dtb_ref[...])
    rt = lax.broadcasted_iota(I32, (tb, tb), 0)
    ct = lax.broadcasted_iota(I32, (tb, tb), 1)
    shift = c.bit_length() - 1
    same = jnp.right_shift(rt, shift) == jnp.right_shift(ct, shift)
    gcum_all = _fdot(jnp.where(same & (rt >= ct), 1.0, 0.0), g_all)
    ri = lax.broadcasted_iota(I32, (c, c), 0)
    ci = lax.broadcasted_iota(I32, (c, c), 1)
    incl = ri >= ci
    strict = ri > ci

    units = [(hh, ch) for hh in range(GDN_HG) for ch in range(nch)]
    qn_h, kn_h, v_h, beta_h, gc_h, grow_h = [], [], [], [], [], []
    for hh in range(GDN_HG):
        h = hg * GDN_HG + hh
        sl = slice(hh * DK, (hh + 1) * DK)
        qn_h.append(_l2norm(qc[:, sl]) * (DK ** -0.5))
        kn_h.append(_l2norm(kc[:, sl]))
        v_h.append(vc[:, sl])
        beta_h.append(jnp.sum(jnp.where(lane == h, beta_all, 0.0), axis=-1, keepdims=True))
        gcol = jnp.sum(jnp.where(lane == N_DN_HEADS + h, gcum_all, 0.0), axis=-1, keepdims=True)
        gb = jnp.broadcast_to(gcol, (tb, LANES))
        gc_h.append(gb)
        grow_h.append(gb.T)
    rows = lambda ch: slice(ch * c, (ch + 1) * c)
    qn = [qn_h[hh][rows(ch)] for hh, ch in units]
    kn = [kn_h[hh][rows(ch)] for hh, ch in units]
    vv = [v_h[hh][rows(ch)] for hh, ch in units]
    beta = [beta_h[hh][rows(ch)] for hh, ch in units]
    gc = [gc_h[hh][rows(ch)] for hh, ch in units]
    grow = [grow_h[hh][0:c, ch * c:(ch + 1) * c] for hh, ch in units]
    eg = [jnp.exp(g) for g in gc]
    kb = [k * b for k, b in zip(kn, beta)]
    vb = [v * b for v, b in zip(vv, beta)]
    kexp = [k * e for k, e in zip(kb, eg)]
    qexp = [q * e for q, e in zip(qn, eg)]
    glast = [g[c - 1:c, :] for g in gc]
    kend = [k * jnp.exp(gl - g) for k, gl, g in zip(kn, glast, gc)]
    sdec = [jnp.exp(gl) for gl in glast]
    ak = [_bdot_nt(jnp.concatenate([a, q], axis=0), k) for a, q, k in zip(kb, qn, kn)]
    decay = [jnp.where(incl, jnp.exp(jnp.where(incl, g[:, :c] - gr, 0.0)), 0.0) for g, gr in zip(gc, grow)]
    a_mat = [jnp.where(strict, x[:c] * d, 0.0) for x, d in zip(ak, decay)]
    qk = [jnp.where(incl, x[c:] * d, 0.0) for x, d in zip(ak, decay)]
    t_inv = _unit_lower_inverses(a_mat)
    uw = [_bdot(ti, jnp.concatenate([v, k], axis=1)) for ti, v, k in zip(t_inv, vb, kexp)]

    s = [s_scr[hh] for hh in range(GDN_HG)]
    outs = [[] for _ in range(GDN_HG)]
    for ch in range(nch):
        idx = [hh * nch + ch for hh in range(GDN_HG)]
        ws = [_bdot(jnp.concatenate([uw[i][:, DV:], qexp[i]], axis=0), s[hh]) for hh, i in enumerate(idx)]
        u = [uw[i][:, :DV] - w[:c] for i, w in zip(idx, ws)]
        o = [w[c:] + _bdot(qk[i], uu) for i, w, uu in zip(idx, ws, u)]
        s = [s[hh] * sdec[i] + _bdot_tn(kend[i], uu) for (hh, i), uu in zip(enumerate(idx), u)]
        for hh in range(GDN_HG):
            outs[hh].append(o[hh])
    dnw = dnw_ref[...]
    for hh in range(GDN_HG):
        s_scr[hh] = s[hh]
        sl = slice(hh * DK, (hh + 1) * DK)
        o = jnp.concatenate(outs[hh], axis=0)
        o = o * lax.rsqrt(jnp.mean(o * o, axis=-1, keepdims=True) + EPS) * dnw
        zh = z_ref[:, sl]
        o_ref[:, sl] = (o * (zh * _sigmoid(zh))).astype(o_ref.dtype)

    @pl.when(t == pl.num_programs(2) - 1)
    def _():
        sout_ref[0] = s_scr[...]


def gdn_prompt(proj, ba, conv_w, a_log, dt_bias, dn_norm, batch, seq):
    tb = GDN_TBLOCK
    nt = seq // tb
    wblk = GDN_HG * DK
    q0, k0, v0, z0 = (COL_QKV_D // wblk, (COL_QKV_D + N_DN_HEADS * DK) // wblk,
                      (COL_QKV_D + 2 * N_DN_HEADS * DK) // wblk, COL_Z // wblk)
    nhg = N_DN_HEADS // GDN_HG
    row = lambda b, t: b * nt + t
    col_spec = lambda c0: pl.BlockSpec((tb, wblk), lambda b, hg, t: (row(b, t), c0 + hg))
    cw_spec = lambda c0: pl.BlockSpec((CONV_W, wblk), lambda b, hg, t: (0, c0 + hg))
    vec = pl.BlockSpec((1, LANES), lambda b, hg, t: (0, 0))
    return pl.pallas_call(
        functools.partial(_gdn_prompt_kernel, tb=tb),
        grid=(batch, nhg, nt),
        in_specs=[col_spec(q0), col_spec(k0), col_spec(v0), col_spec(z0),
                  pl.BlockSpec((tb, LANES), lambda b, hg, t: (row(b, t), 0)),
                  cw_spec(0), cw_spec(nhg), cw_spec(2 * nhg), vec, vec, vec],
        out_specs=[pl.BlockSpec((tb, wblk), lambda b, hg, t: (row(b, t), hg)),
                   pl.BlockSpec((1, GDN_HG, DK, DV), lambda b, hg, t: (b, hg, 0, 0))],
        out_shape=[jax.ShapeDtypeStruct((batch * seq, DN_V), BF16),
                   jax.ShapeDtypeStruct((batch, N_DN_HEADS, DK, DV), F32)],
        scratch_shapes=[pltpu.VMEM((tb + SUBLANES, wblk), F32)] * 3 + [pltpu.VMEM((GDN_HG, DK, DV), F32)],
        compiler_params=_cp(("parallel", "parallel", "arbitrary")),
        name="gdn_prompt",
    )(proj, proj, proj, proj, ba, conv_w, conv_w, conv_w, _pad_a_lanes(a_log), _pad_a_lanes(dt_bias),
      dn_norm.reshape(1, DV))


def _gdn_sample_kernel(xp_ref, z_ref, ba_ref, cw_ref, alog_ref, dtb_ref, dnw_ref, s0_ref, o_ref, s_ref, *, t_new):
    bg = xp_ref.shape[0]
    rows = xp_ref.shape[1]
    ri = lax.broadcasted_iota(I32, (rows, LANES), 0)
    eye = jnp.where(lax.broadcasted_iota(I32, (LANES, LANES), 0) == lax.broadcasted_iota(I32, (LANES, LANES), 1),
                    1.0, 0.0)
    dnw = dnw_ref[...]

    def per_batch(b, carry):
        xp = xp_ref[b]
        y = None
        for jj in range(CONV_W):
            sh = xp if jj == 0 else pltpu.roll(xp, rows - jj, axis=0)
            term = sh * cw_ref[jj:jj + 1, :]
            y = term if y is None else y + term
        y = y * _sigmoid(y)
        ba = ba_ref[b]
        beta_all = _sigmoid(ba)
        g_all = -jnp.exp(alog_ref[...]) * _softplus(ba + dtb_ref[...])
        z = z_ref[b]
        eg_all = jnp.exp(g_all)
        for h0 in range(0, N_DN_HEADS, GDN_SAMPLE_HEADS):
            hs = list(range(h0, h0 + GDN_SAMPLE_HEADS))
            qn = [_l2norm(y[:, h * DK:(h + 1) * DK]) * (DK ** -0.5) for h in hs]
            kn = [_l2norm(y[:, (N_DN_HEADS + h) * DK:(N_DN_HEADS + h + 1) * DK]) for h in hs]
            vh = [y[:, (2 * N_DN_HEADS + h) * DK:(2 * N_DN_HEADS + h + 1) * DK] for h in hs]
            kq = [jnp.where(ri < t_new, k, pltpu.roll(q, t_new, axis=0)) for k, q in zip(kn, qn)]
            kq_t = [lax.dot_general(eye, x, (((1,), (1,)), ((), ())), preferred_element_type=F32,
                                    precision=lax.Precision.HIGHEST) for x in kq]
            s = [s0_ref[b, h] for h in hs]
            o_rows = [jnp.zeros((rows, DV), F32) for _ in hs]
            for tt in range(t_new):
                kcol = [x[:, tt:tt + 1] for x in kq_t]
                qcol = [x[:, t_new + tt:t_new + tt + 1] for x in kq_t]
                beta = [beta_all[tt:tt + 1, h:h + 1] for h in hs]
                eg = [eg_all[tt:tt + 1, N_DN_HEADS + h:N_DN_HEADS + h + 1] for h in hs]
                s = [a * e for a, e in zip(s, eg)]
                ks = [jnp.sum(a * k, axis=0, keepdims=True) for a, k in zip(s, kcol)]
                delta = [bt * (v[tt:tt + 1, :] - x) for bt, v, x in zip(beta, vh, ks)]
                s = [a + k * d for a, k, d in zip(s, kcol, delta)]
                o_t = [jnp.sum(a * q, axis=0, keepdims=True) for a, q in zip(s, qcol)]
                o_rows = [jnp.where(ri == tt, x, acc) for x, acc in zip(o_t, o_rows)]
            for h, a, orow in zip(hs, s, o_rows):
                s_ref[b, h] = a
                o = orow * lax.rsqrt(jnp.mean(orow * orow, axis=-1, keepdims=True) + EPS) * dnw
                zh = z[:, h * DV:(h + 1) * DV]
                o_ref[b, :, h * DV:(h + 1) * DV] = (o * (zh * _sigmoid(zh))).astype(o_ref.dtype)
        return carry

    lax.fori_loop(0, bg, per_batch, 0)


def gdn_sample(xp, z, ba, conv_w, a_log, dt_bias, dn_norm, s0, t_new, bg=4):
    b = xp.shape[0]
    rows = xp.shape[1]
    full = lambda shape: pl.BlockSpec(shape, lambda i: (0,) * len(shape))
    return pl.pallas_call(
        functools.partial(_gdn_sample_kernel, t_new=t_new),
        grid=(b // bg,),
        in_specs=[pl.BlockSpec((bg, rows, CONV_DIM), lambda i: (i, 0, 0)),
                  pl.BlockSpec((bg, rows, DN_V), lambda i: (i, 0, 0)),
                  pl.BlockSpec((bg, rows, LANES), lambda i: (i, 0, 0)),
                  full((CONV_W, CONV_DIM)), full((1, LANES)), full((1, LANES)), full((1, DV)),
                  pl.BlockSpec((bg, N_DN_HEADS, DK, DV), lambda i: (i, 0, 0, 0))],
        out_specs=[pl.BlockSpec((bg, rows, DN_V), lambda i: (i, 0, 0)),
                   pl.BlockSpec((bg, N_DN_HEADS, DK, DV), lambda i: (i, 0, 0, 0))],
        out_shape=[jax.ShapeDtypeStruct((b, rows, DN_V), BF16),
                   jax.ShapeDtypeStruct((b, N_DN_HEADS, DK, DV), F32)],
        compiler_params=_cp(("parallel",)),
        name="gdn_sample",
    )(xp, z, ba, conv_w, _pad_a_lanes(a_log), _pad_a_lanes(dt_bias), dn_norm.reshape(1, DV), s0)


def _router_kernel(x_ref, g_ref, wr_ref, br_ref, mi_ref, mf_ref, cnt_ref, run_ref):
    i = pl.program_id(0)
    tm = x_ref.shape[0]

    @pl.when(i == 0)
    def _():
        run_ref[...] = jnp.zeros(run_ref.shape, F32)

    x = x_ref[...]
    xn = x * lax.rsqrt(jnp.mean(x * x, axis=-1, keepdims=True) + EPS) * g_ref[...]
    lane = lax.broadcasted_iota(I32, (tm, LANES), 1)
    lane_f = lane.astype(F32)
    logits = _fdot(xn, wr_ref[...]) + br_ref[...]
    work = jnp.where(lane < N_EXPERTS, logits, -jnp.inf)
    vals, idxs, hots = [], [], []
    for _ in range(TOP_K):
        mx = jnp.max(work, axis=-1, keepdims=True)
        idx_f = jnp.min(jnp.where(work == mx, lane_f, float(LANES)), axis=-1, keepdims=True)
        idx = idx_f.astype(I32)
        hot = lane == idx
        work = jnp.where(hot, -jnp.inf, work)
        vals.append(mx)
        idxs.append(idx)
        hots.append(hot)
    exps = [jnp.exp(v - vals[0]) for v in vals]
    den = exps[0] + exps[1] + exps[2] + exps[3]
    member = jnp.where(hots[0] | hots[1] | hots[2] | hots[3], 1.0, 0.0)
    ri = lax.broadcasted_iota(I32, (tm, tm), 0)
    ci = lax.broadcasted_iota(I32, (tm, tm), 1)
    before = _bdot(jnp.where(ri > ci, 1.0, 0.0), member) + run_ref[0:1, :]
    run_ref[...] = run_ref[...] + jnp.sum(member, axis=0, keepdims=True)
    mi = jnp.zeros((tm, LANES), I32)
    mf = jnp.zeros((tm, LANES), F32)
    for k in range(TOP_K):
        rank = jnp.sum(jnp.where(hots[k], before, 0.0), axis=-1, keepdims=True).astype(I32)
        mi = jnp.where(lane == k, idxs[k], mi)
        mi = jnp.where(lane == TOP_K + k, rank, mi)
        mf = jnp.where(lane == k, exps[k] / den, mf)
    mi_ref[...] = mi
    mf_ref[...] = mf
    cnt_ref[...] = run_ref[...]


def router(x, gamma, w_router, b_router):
    m, d = x.shape
    tm = TOK_TILE
    wr = jnp.zeros((d, LANES), F32).at[:, :N_EXPERTS].set(w_router)
    br = jnp.zeros((1, LANES), F32).at[0, :N_EXPERTS].set(b_router)
    return pl.pallas_call(
        _router_kernel,
        grid=(m // tm,),
        in_specs=[pl.BlockSpec((tm, d), lambda i: (i, 0)), pl.BlockSpec((1, d), lambda i: (0, 0)),
                  pl.BlockSpec((d, LANES), lambda i: (0, 0)), pl.BlockSpec((1, LANES), lambda i: (0, 0))],
        out_specs=[pl.BlockSpec((tm, LANES), lambda i: (i, 0)), pl.BlockSpec((tm, LANES), lambda i: (i, 0)),
                   pl.BlockSpec((SUBLANES, LANES), lambda i: (0, 0))],
        out_shape=[jax.ShapeDtypeStruct((m, LANES), I32), jax.ShapeDtypeStruct((m, LANES), F32),
                   jax.ShapeDtypeStruct((SUBLANES, LANES), F32)],
        scratch_shapes=[pltpu.VMEM((SUBLANES, LANES), F32)],
        compiler_params=_cp(("arbitrary",)),
        name="router",
    )(x, gamma.reshape(1, d), wr, br)


def _zero_kernel(sub_ref, o_ref):
    del sub_ref
    o_ref[...] = jnp.zeros(o_ref.shape, o_ref.dtype)


def zero_partial_subblocks(sub_idx, n_rows, d):
    return pl.pallas_call(
        _zero_kernel,
        grid_spec=pltpu.PrefetchScalarGridSpec(
            num_scalar_prefetch=1, grid=(sub_idx.shape[0],), in_specs=[],
            out_specs=pl.BlockSpec((MOE_SUB, d), lambda e, sub: (sub[e], 0))),
        out_shape=jax.ShapeDtypeStruct((n_rows, d), F32),
        compiler_params=_cp(("arbitrary",)),
        name="moe_zero_tails",
    )(sub_idx)


def _row_copy(src_ref, src_row, dst_ref, dst_row, sem):
    return pltpu.make_async_copy(src_ref.at[pl.ds(src_row, 1), :], dst_ref.at[pl.ds(dst_row, 1), :], sem)


def _load_dest(dest_ref, dest_smem, sem):
    cp = pltpu.make_async_copy(dest_ref, dest_smem, sem)
    cp.start()
    cp.wait()


def _dest_of(dest_smem, r, k):
    per_row = LANES // TOP_K
    return dest_smem[r // per_row, (r % per_row) * TOP_K + k]


def _dispatch_kernel(dest_ref, x_ref, xs_in_ref, xs_ref, dest_smem, sem_idx, sem):
    del xs_in_ref
    tm = x_ref.shape[0]
    _load_dest(dest_ref, dest_smem, sem_idx)

    def issue(r, c):
        for k in range(TOP_K):
            _row_copy(x_ref, r, xs_ref, _dest_of(dest_smem, r, k), sem).start(priority=k % 2)
        return c

    lax.fori_loop(0, tm, issue, 0)

    def drain(r, c):
        for k in range(TOP_K):
            _row_copy(x_ref, 0, xs_ref, 0, sem).wait()
        return c

    lax.fori_loop(0, tm, drain, 0)


def dispatch(x, dest2d, xs):
    m, d = x.shape
    tm = TOK_TILE
    drows = tm * TOP_K // LANES
    return pl.pallas_call(
        _dispatch_kernel,
        grid=(m // tm,),
        in_specs=[pl.BlockSpec((drows, LANES), lambda i: (i, 0)),
                  pl.BlockSpec((tm, d), lambda i: (i, 0)),
                  pl.BlockSpec(memory_space=pl.ANY)],
        out_specs=pl.BlockSpec(memory_space=pl.ANY),
        out_shape=jax.ShapeDtypeStruct(xs.shape, xs.dtype),
        scratch_shapes=[pltpu.SMEM((drows, LANES), I32), pltpu.SemaphoreType.DMA, pltpu.SemaphoreType.DMA],
        input_output_aliases={2: 0},
        compiler_params=_cp(("arbitrary",)),
        name="moe_dispatch",
    )(dest2d, x, xs)


def _expert_kernel(blk_ref, bexp_ref, nsub_ref, last_ref, x0_ref, x1_ref, x2_ref, x3_ref, g_ref, wg_ref, wu_ref,
                   bg_ref, bu_ref, wo_ref, bo_ref, o_ref, xn_ref, wgb_ref, wub_ref, wob_ref):
    del blk_ref, bexp_ref, last_ref
    b = pl.program_id(0)
    f = pl.program_id(1)
    n = nsub_ref[b]
    x_refs = (x0_ref, x1_ref, x2_ref, x3_ref)

    @pl.when(f == 0)
    def _():
        for s, x_ref in enumerate(x_refs):
            rows = slice(s * MOE_SUB, (s + 1) * MOE_SUB)

            @pl.when(s < n)
            def _():
                x = x_ref[...]
                xn = x * lax.rsqrt(jnp.mean(x * x, axis=-1, keepdims=True) + EPS) * g_ref[...]
                xn_ref[rows, :] = xn.astype(BF16)
                o_ref[rows, :] = jnp.broadcast_to(bo_ref[...], (MOE_SUB, o_ref.shape[1]))

            @pl.when((n > 0) & (s >= n))
            def _():
                o_ref[rows, :] = jnp.zeros((MOE_SUB, o_ref.shape[1]), F32)

    @pl.when(n > 0)
    def _():
        wgb_ref[...] = wg_ref[...].astype(BF16)
        wub_ref[...] = wu_ref[...].astype(BF16)
        wob_ref[...] = wo_ref[...].astype(BF16)

    def sub(s, c):
        rows = pl.ds(pl.multiple_of(s * MOE_SUB, MOE_SUB), MOE_SUB)
        xs = xn_ref[rows, :]
        gate = jnp.dot(xs, wgb_ref[...], preferred_element_type=F32) + bg_ref[...]
        up = jnp.dot(xs, wub_ref[...], preferred_element_type=F32) + bu_ref[...]
        gate = jnp.minimum(gate, SWIGLU_LIMIT)
        up = jnp.clip(up, -SWIGLU_LIMIT, SWIGLU_LIMIT)
        act = (up + 1.0) * gate * _sigmoid(SWIGLU_ALPHA * gate)
        o_ref[rows, :] += jnp.dot(act.astype(BF16), wob_ref[...], preferred_element_type=F32)
        return c

    lax.fori_loop(0, n, sub, 0)


def experts(xs, gamma, w_in, b_in, w_out, b_out, blk, bexp, nsub, last_sub):
    n_rows, d = xs.shape
    nb = blk.shape[0]
    nf = D_FF // MOE_TF
    per = MOE_ROWS // MOE_SUB
    assert per == 4

    def x_spec(s):
        return pl.BlockSpec((MOE_SUB, d),
                            lambda b, f, blk, bexp, nsub, last: (jnp.minimum(per * blk[b] + s, last[b]), 0))

    wsel = lambda b, f, blk, bexp, nsub, last: bexp[b]
    in_specs = [x_spec(0), x_spec(1), x_spec(2), x_spec(3),
                pl.BlockSpec((1, d), lambda b, f, *_: (0, 0)),
                pl.BlockSpec((None, d, MOE_TF), lambda b, f, *p: (wsel(b, f, *p), 0, f)),
                pl.BlockSpec((None, d, MOE_TF), lambda b, f, *p: (wsel(b, f, *p), 0, nf + f)),
                pl.BlockSpec((None, 1, MOE_TF), lambda b, f, *p: (wsel(b, f, *p), 0, f)),
                pl.BlockSpec((None, 1, MOE_TF), lambda b, f, *p: (wsel(b, f, *p), 0, nf + f)),
                pl.BlockSpec((None, MOE_TF, d), lambda b, f, *p: (wsel(b, f, *p), f, 0)),
                pl.BlockSpec((None, 1, d), lambda b, f, *p: (wsel(b, f, *p), 0, 0))]
    return pl.pallas_call(
        _expert_kernel,
        grid_spec=pltpu.PrefetchScalarGridSpec(
            num_scalar_prefetch=4, grid=(nb, nf), in_specs=in_specs,
            out_specs=pl.BlockSpec((MOE_ROWS, d), lambda b, f, blk, bexp, nsub, last: (blk[b], 0)),
            scratch_shapes=[pltpu.VMEM((MOE_ROWS, d), BF16), pltpu.VMEM((d, MOE_TF), BF16),
                            pltpu.VMEM((d, MOE_TF), BF16), pltpu.VMEM((MOE_TF, d), BF16)]),
        out_shape=jax.ShapeDtypeStruct((n_rows, d), F32),
        compiler_params=_cp(("arbitrary", "arbitrary")),
        name="moe_experts",
    )(blk, bexp, nsub, last_sub, xs, xs, xs, xs, gamma.reshape(1, d), w_in, w_in,
      b_in.reshape(N_EXPERTS, 1, 2 * D_FF), b_in.reshape(N_EXPERTS, 1, 2 * D_FF), w_out,
      b_out.reshape(N_EXPERTS, 1, d))


def _combine_kernel(dest_ref, x_ref, mf_ref, ys_ref, o_ref, buf, dest_smem, sem_idx, sem):
    tm = x_ref.shape[0]
    _load_dest(dest_ref, dest_smem, sem_idx)

    def issue(r, c):
        for k in range(TOP_K):
            _row_copy(ys_ref, _dest_of(dest_smem, r, k), buf.at[k], r, sem).start(priority=k % 2)
        return c

    lax.fori_loop(0, tm, issue, 0)

    def drain(r, c):
        for k in range(TOP_K):
            _row_copy(ys_ref, 0, buf.at[k], 0, sem).wait()
        return c

    lax.fori_loop(0, tm, drain, 0)
    gates = mf_ref[...]
    acc = x_ref[...]
    for k in range(TOP_K):
        acc = acc + gates[:, k:k + 1] * buf[k]
    o_ref[...] = acc


def combine(x, mf, dest2d, ys):
    m, d = x.shape
    tm = TOK_TILE
    drows = tm * TOP_K // LANES
    return pl.pallas_call(
        _combine_kernel,
        grid=(m // tm,),
        in_specs=[pl.BlockSpec((drows, LANES), lambda i: (i, 0)),
                  pl.BlockSpec((tm, d), lambda i: (i, 0)),
                  pl.BlockSpec((tm, LANES), lambda i: (i, 0)),
                  pl.BlockSpec(memory_space=pl.ANY)],
        out_specs=pl.BlockSpec((tm, d), lambda i: (i, 0)),
        out_shape=jax.ShapeDtypeStruct((m, d), F32),
        scratch_shapes=[pltpu.VMEM((TOP_K, tm, d), F32), pltpu.SMEM((drows, LANES), I32),
                        pltpu.SemaphoreType.DMA, pltpu.SemaphoreType.DMA],
        compiler_params=_cp(("arbitrary",)),
        name="moe_combine",
    )(dest2d, x, mf, ys)


def routed_experts_residual(x, gamma, w_router, b_router, w_exp_in, b_exp_in, w_exp_out, b_exp_out):
    m, d = x.shape
    mi, mf, cnt = router(x, gamma, w_router, b_router)
    idx = mi[:, :TOP_K]
    rank = mi[:, TOP_K:2 * TOP_K]
    counts = cnt[0, :N_EXPERTS].astype(I32)
    padded = (counts + MOE_ROWS - 1) // MOE_ROWS * MOE_ROWS
    pad_end = jnp.cumsum(padded)
    pad_start = pad_end - padded
    onehot = idx[:, :, None] == jnp.arange(N_EXPERTS, dtype=I32)[None, None, :]
    dest = jnp.sum(jnp.where(onehot, pad_start[None, None, :], 0), axis=-1) + rank
    dest2d = dest.reshape(m * TOP_K // LANES, LANES)
    n_blocks = (m * TOP_K + N_EXPERTS * (MOE_ROWS - 1) + MOE_ROWS - 1) // MOE_ROWS
    used = pad_end[-1] // MOE_ROWS
    bids = jnp.arange(n_blocks, dtype=I32)
    blk = jnp.minimum(bids, jnp.maximum(used - 1, 0))
    bexp = jnp.minimum(jnp.sum((pad_end[None, :] <= (blk * MOE_ROWS)[:, None]).astype(I32), axis=1), N_EXPERTS - 1)
    fill_end = pad_start + counts
    valid_rows = jnp.clip(fill_end[bexp] - blk * MOE_ROWS, 0, MOE_ROWS)
    nsub = jnp.where(bids < used, (valid_rows + MOE_SUB - 1) // MOE_SUB, 0).astype(I32)
    per = MOE_ROWS // MOE_SUB
    last_sub = (per * blk + jnp.maximum(nsub, 1) - 1).astype(I32)
    n_rows = n_blocks * MOE_ROWS
    tail_sub = jnp.minimum(fill_end // MOE_SUB, n_rows // MOE_SUB - 1).astype(I32)
    xs = zero_partial_subblocks(tail_sub, n_rows, d)
    xs = dispatch(x, dest2d, xs)
    ys = experts(xs, gamma, w_exp_in, b_exp_in, w_exp_out, b_exp_out, blk, bexp, nsub, last_sub)
    return combine(x, mf, dest2d, ys)


def _ep_plain(dots, extras):
    return dots[0]


def _ep_sigmoid(dots, extras):
    return _sigmoid(dots[0])


def _ep_merge(dots, extras):
    return extras[0] * dots[0] + extras[1] * dots[1]


def _ep_residual(dots, extras):
    return extras[0] + dots[0]


def _ep_ple(dots, extras):
    return extras[0] + _sigmoid(dots[0]) * dots[1]


def kernel(x_prompt, x_sample, cache_k, cache_v, state_conv, state_ssm, p_prompt, p_sample, w_in, sinks, conv_w, a_log, dt_bias, dn_norm, w_attn_branch, w_dn_branch, w_out, w_router, b_router, w_exp_in, b_exp_in, w_exp_out, b_exp_out, w_ple_proj, w_ple_gate, norm_mix, norm_moe, norm_ple, norm_final):
    depth = w_in.shape[0]
    bp, seq, d = x_prompt.shape
    bs, t_new, _ = x_sample.shape
    w_cache = cache_k.shape[2]
    mp, ms = bp * seq, bs * t_new
    x = jnp.concatenate([x_prompt.reshape(mp, d), x_sample.reshape(ms, d)], axis=0)
    p_all = jnp.concatenate([p_prompt.reshape(depth, mp, -1), p_sample.reshape(depth, ms, -1)], axis=1)
    m = mp + ms
    tm_big = 1536 if m % 1536 == 0 else 512
    tm_mid = 768 if m % 768 == 0 else 512
    outs = {k: [] for k in ("kp", "vp", "cp", "sp", "ks", "vs", "cs", "ss")}
    for i in range(depth):
        h0 = rms_rows(x, norm_mix[i], BF16)
        proj = fused_mm([(h0, w_in[i], 0)], [], _ep_plain, COL_BA, F32, tm_big, 512, "proj_main")
        w_ba = jnp.zeros((d, LANES), F32).at[:, :2 * N_DN_HEADS].set(w_in[i][:, COL_BA:COL_GATES])
        ba = fused_mm([(h0, w_ba, 0)], [], _ep_plain, LANES, F32, tm_big, LANES, "proj_ba")
        gates = fused_mm([(h0, w_in[i][:, COL_GATES:], 0)], [], _ep_sigmoid, 2 * d, F32, tm_big, 512, "proj_gates")

        o_a_p = swa_prompt(proj, sinks[i], bp, seq)
        ps = proj[mp:]
        q_s = ps[:, :ATT_Q].reshape(bs, t_new, N_KV_HEADS, GROUP, HEAD_DIM).transpose(0, 2, 3, 1, 4)
        q_s = q_s.reshape(bs, N_KV_HEADS, GROUP * t_new, HEAD_DIM)
        k_s = ps[:, ATT_Q:ATT_Q + ATT_KV].reshape(bs, t_new, N_KV_HEADS, HEAD_DIM)
        v_s = ps[:, ATT_Q + ATT_KV:COL_QKV_D].reshape(bs, t_new, N_KV_HEADS, HEAD_DIM)
        o_a_s = swa_sample(q_s, k_s.transpose(0, 2, 1, 3), v_s.transpose(0, 2, 1, 3),
                           cache_k[i].transpose(0, 2, 1, 3), cache_v[i].transpose(0, 2, 1, 3), sinks[i])
        o_a_s = o_a_s.reshape(bs, N_KV_HEADS, GROUP, t_new, HEAD_DIM).transpose(0, 3, 1, 2, 4).reshape(ms, ATT_Q)
        o_a = jnp.concatenate([o_a_p, o_a_s.astype(BF16)], axis=0)
        tail = lambda n, c0, c1: jnp.stack(
            [lax.slice(proj, ((b + 1) * seq - n, c0), ((b + 1) * seq, c1)) for b in range(bp)])
        outs["kp"].append(tail(WINDOW, ATT_Q, ATT_Q + ATT_KV).reshape(bp, WINDOW, N_KV_HEADS, HEAD_DIM))
        outs["vp"].append(tail(WINDOW, ATT_Q + ATT_KV, COL_QKV_D).reshape(bp, WINDOW, N_KV_HEADS, HEAD_DIM))
        outs["ks"].append(jnp.concatenate([cache_k[i], k_s], axis=1)[:, -w_cache:])
        outs["vs"].append(jnp.concatenate([cache_v[i], v_s], axis=1)[:, -w_cache:])

        o_d_p, s_p = gdn_prompt(proj, ba, conv_w[i], a_log[i], dt_bias[i], dn_norm[i], bp, seq)
        qkv_s = ps[:, COL_QKV_D:COL_Z].reshape(bs, t_new, CONV_DIM)
        rows = SUBLANES
        pad_rows = rows - (CONV_W - 1) - t_new
        xp_s = jnp.concatenate([state_conv[i], qkv_s, jnp.zeros((bs, pad_rows, CONV_DIM), F32)], axis=1)
        tok_pad = lambda a: jnp.pad(a.reshape(bs, t_new, -1), ((0, 0), (0, rows - t_new), (0, 0)))
        o_d_s, s_s = gdn_sample(xp_s, tok_pad(ps[:, COL_Z:COL_BA]), tok_pad(ba[mp:]), conv_w[i], a_log[i],
                                dt_bias[i], dn_norm[i], state_ssm[i], t_new)
        o_d = jnp.concatenate([o_d_p, o_d_s[:, :t_new].reshape(ms, DN_V)], axis=0)
        outs["cp"].append(tail(CONV_W - 1, COL_QKV_D, COL_Z))
        outs["cs"].append(jnp.concatenate([state_conv[i], qkv_s], axis=1)[:, -(CONV_W - 1):])
        outs["sp"].append(s_p)
        outs["ss"].append(s_s)

        merged = _merge_call(o_a, o_d, w_attn_branch[i], w_dn_branch[i], gates, tm_mid)
        x = fused_mm([(merged, w_out[i], 0)], [x], _ep_residual, d, F32, tm_mid, 512, "out_proj")

        x = routed_experts_residual(x, norm_moe[i], w_router[i], b_router[i], w_exp_in[i], b_exp_in[i],
                                    w_exp_out[i], b_exp_out[i])

        hp = rms_rows(x, norm_ple[i], BF16)
        x = fused_mm([(hp, w_ple_gate[i], 0), (p_all[i], w_ple_proj[i], 0)], [x], _ep_ple, d, F32, tm_mid, 512,
                     "ple")
    y = rms_rows(x, norm_final, F32)
    st = lambda k: jnp.stack(outs[k])
    return (y[:mp].reshape(bp, seq, d), y[mp:].reshape(bs, t_new, d), st("kp"), st("vp"), st("cp"), st("sp"),
            st("ks"), st("vs"), st("cs"), st("ss"))


def _merge_call(o_a, o_d, w_a, w_b, gates, tm):
    m = o_a.shape[0]
    d = w_a.shape[1]
    tn = 512
    nj = d // tn
    ka, kb = o_a.shape[1], o_d.shape[1]

    def body(a_ref, b_ref, wa_ref, wb_ref, ga_ref, gb_ref, o_ref, wab_ref, wbb_ref):
        @pl.when(pl.program_id(1) == 0)
        def _():
            wab_ref[...] = wa_ref[...].astype(BF16)
            wbb_ref[...] = wb_ref[...].astype(BF16)

        da = jnp.dot(a_ref[...], wab_ref[...], preferred_element_type=F32)
        db = jnp.dot(b_ref[...], wbb_ref[...], preferred_element_type=F32)
        o_ref[...] = (ga_ref[...] * da + gb_ref[...] * db).astype(o_ref.dtype)

    return pl.pallas_call(
        body,
        grid=(nj, m // tm),
        in_specs=[pl.BlockSpec((tm, ka), lambda j, i: (i, 0)), pl.BlockSpec((tm, kb), lambda j, i: (i, 0)),
                  pl.BlockSpec((ka, tn), lambda j, i: (0, j)), pl.BlockSpec((kb, tn), lambda j, i: (0, j)),
                  pl.BlockSpec((tm, tn), lambda j, i: (i, j)), pl.BlockSpec((tm, tn), lambda j, i: (i, nj + j))],
        out_specs=pl.BlockSpec((tm, tn), lambda j, i: (i, j)),
        out_shape=jax.ShapeDtypeStruct((m, d), BF16),
        scratch_shapes=[pltpu.VMEM((ka, tn), BF16), pltpu.VMEM((kb, tn), BF16)],
        compiler_params=_cp(("arbitrary", "arbitrary")),
        name="branch_merge",
    )(o_a, o_d, w_a, w_b, gates, gates)
```

```python
import functools

import jax
import jax.numpy as jnp
from jax import lax
from jax.experimental import pallas as pl
from jax.experimental.pallas import tpu as pltpu

F32 = jnp.float32
BF16 = jnp.bfloat16
I32 = jnp.int32

D_MODEL = 2048
N_Q_HEADS = 16
N_KV_HEADS = 4
GROUP = N_Q_HEADS // N_KV_HEADS
HEAD_DIM = 64
WINDOW = 128
N_DN_HEADS = 16
DK = 128
DV = 128
CONV_W = 4
CONV_DIM = N_DN_HEADS * (2 * DK + DV)
N_EXPERTS = 32
TOP_K = 4
D_FF = D_MODEL
SWIGLU_LIMIT = 7.0
SWIGLU_ALPHA = 1.702
EPS = 1e-6
ATT_Q = N_Q_HEADS * HEAD_DIM
ATT_KV = N_KV_HEADS * HEAD_DIM
DN_V = N_DN_HEADS * DV

COL_QKV_D = ATT_Q + 2 * ATT_KV
COL_Z = COL_QKV_D + CONV_DIM
COL_BA = COL_Z + DN_V
COL_GATES = COL_BA + 2 * N_DN_HEADS

LANES = 128
SUBLANES = 8
VMEM_LIMIT_BYTES = 56 * 2 ** 20

GDN_CHUNK = 64
GDN_TBLOCK = 256
GDN_HG = 4
GDN_SAMPLE_HEADS = 8
MOE_ROWS = 1024
MOE_SUB = 256
MOE_TF = 256
TOK_TILE = 256
NEG_BIG = -1e30


def _cp(sem, vmem=VMEM_LIMIT_BYTES):
    return pltpu.CompilerParams(dimension_semantics=sem, vmem_limit_bytes=vmem)


def _sigmoid(x):
    return 1.0 / (1.0 + jnp.exp(-x))


def _softplus(x):
    return jnp.maximum(x, 0.0) + jnp.log1p(jnp.exp(-jnp.abs(x)))


def _bdot(a, b):
    return jnp.dot(a.astype(BF16), b.astype(BF16), preferred_element_type=F32)


def _bdot_nt(a, b):
    return lax.dot_general(a.astype(BF16), b.astype(BF16), (((1,), (1,)), ((), ())),
                           preferred_element_type=F32)


def _bdot_tn(a, b):
    return lax.dot_general(a.astype(BF16), b.astype(BF16), (((0,), (0,)), ((), ())),
                           preferred_element_type=F32)


def _fdot(a, b):
    return jnp.dot(a, b, preferred_element_type=F32, precision=lax.Precision.HIGHEST)


def _parts(x):
    return tuple(x) if isinstance(x, (tuple, list)) else (x,)


def _tile_starts(parts, tm):
    starts, s = [], 0
    for p in parts:
        assert p.shape[0] % tm == 0
        starts.append(s)
        s += p.shape[0] // tm
    return tuple(starts), s


def _part_spec(part, start, tm, cols, col_of, tile_arg):
    n = part.shape[0] // tm

    def index_map(*g):
        return (jnp.clip(g[tile_arg] - start, 0, n - 1), col_of(*g))

    return pl.BlockSpec((tm, cols), index_map)


def _select_part(refs, starts, i):
    v = refs[0][...]
    for r, s in zip(refs[1:], starts[1:]):
        v = jnp.where(i >= s, r[...], v)
    return v


def _rms_kernel(*refs, in_starts, out_starts):
    i = pl.program_id(0)
    x_refs = refs[:len(in_starts)]
    g_ref = refs[len(in_starts)]
    o_refs = refs[len(in_starts) + 1:]
    x = _select_part(x_refs, in_starts, i)
    y = x * lax.rsqrt(jnp.mean(x * x, axis=-1, keepdims=True) + EPS) * g_ref[...]
    if len(o_refs) == 1:
        o_refs[0][...] = y.astype(o_refs[0].dtype)
        return
    bounds = list(out_starts[1:]) + [None]
    for o_ref, lo, hi in zip(o_refs, out_starts, bounds):
        cond = (i >= lo) if hi is None else ((i >= lo) & (i < hi))

        @pl.when(cond)
        def _():
            o_ref[...] = y.astype(o_ref.dtype)


def rms_rows(x, gamma, out_dtype, tm=512, split_rows=None):
    parts = _parts(x)
    d = parts[0].shape[1]
    in_starts, n_tiles = _tile_starts(parts, tm)
    in_specs = [_part_spec(p, st, tm, d, lambda i: 0, 0) for p, st in zip(parts, in_starts)]
    in_specs.append(pl.BlockSpec((1, d), lambda i: (0, 0)))
    if split_rows is None:
        out_starts = (0,)
        out_specs = pl.BlockSpec((tm, d), lambda i: (i, 0))
        out_shape = jax.ShapeDtypeStruct((n_tiles * tm, d), out_dtype)
    else:
        shapes = [jax.ShapeDtypeStruct((r, d), out_dtype) for r in split_rows]
        out_starts, total = _tile_starts(shapes, tm)
        assert total == n_tiles
        out_specs = [_part_spec(sh, st, tm, d, lambda i: 0, 0) for sh, st in zip(shapes, out_starts)]
        out_shape = shapes
    return pl.pallas_call(
        functools.partial(_rms_kernel, in_starts=in_starts, out_starts=tuple(out_starts)),
        grid=(n_tiles,),
        in_specs=in_specs,
        out_specs=out_specs,
        out_shape=out_shape,
        compiler_params=_cp(("arbitrary",)),
        name="rms_rows",
    )(*parts, gamma.reshape(1, d))


def _fused_mm_kernel(*refs, a_starts, e_starts, epilogue):
    i = pl.program_id(1)
    pos = 0
    a_groups = []
    for st in a_starts:
        a_groups.append(refs[pos:pos + len(st)])
        pos += len(st)
    w_refs = refs[pos:pos + len(a_starts)]
    pos += len(a_starts)
    e_groups = []
    for st in e_starts:
        e_groups.append(refs[pos:pos + len(st)])
        pos += len(st)
    o_ref = refs[pos]
    wb_refs = refs[pos + 1:]

    @pl.when(i == 0)
    def _():
        for w_ref, wb_ref in zip(w_refs, wb_refs):
            wb_ref[...] = w_ref[...].astype(BF16)

    dots = [jnp.dot(_select_part(g, st, i).astype(BF16), wb_ref[...], preferred_element_type=F32)
            for g, st, wb_ref in zip(a_groups, a_starts, wb_refs)]
    extras = [_select_part(g, st, i) for g, st in zip(e_groups, e_starts)]
    o_ref[...] = epilogue(dots, extras).astype(o_ref.dtype)


def fused_mm(pairs, extras, epilogue, n_out, out_dtype, tm, tn, name):
    in_specs, args, scratch, a_starts, e_starts = [], [], [], [], []
    n_tiles = None
    for a, _, _ in pairs:
        parts = _parts(a)
        starts, total = _tile_starts(parts, tm)
        n_tiles = total if n_tiles is None else n_tiles
        assert total == n_tiles
        a_starts.append(starts)
        for p, st in zip(parts, starts):
            in_specs.append(_part_spec(p, st, tm, p.shape[1], lambda j, i: 0, 1))
            args.append(p)
    for a, w, off in pairs:
        k = _parts(a)[0].shape[1]
        assert off % tn == 0 and w.shape[0] == k and n_out % tn == 0
        in_specs.append(pl.BlockSpec((k, tn), functools.partial(lambda j, i, o: (0, j + o), o=off // tn)))
        args.append(w)
        scratch.append(pltpu.VMEM((k, tn), BF16))
    for e in extras:
        parts = _parts(e)
        starts, total = _tile_starts(parts, tm)
        assert total == n_tiles
        e_starts.append(starts)
        for p, st in zip(parts, starts):
            in_specs.append(_part_spec(p, st, tm, tn, lambda j, i: j, 1))
            args.append(p)
    kern = functools.partial(_fused_mm_kernel, a_starts=tuple(a_starts), e_starts=tuple(e_starts),
                             epilogue=epilogue)
    return pl.pallas_call(
        kern,
        grid=(n_out // tn, n_tiles),
        in_specs=in_specs,
        out_specs=pl.BlockSpec((tm, tn), lambda j, i: (i, j)),
        out_shape=jax.ShapeDtypeStruct((n_tiles * tm, n_out), out_dtype),
        scratch_shapes=scratch,
        compiler_params=_cp(("arbitrary", "arbitrary")),
        name=name,
    )(*args)


def _swa_prompt_kernel(sink_ref, q_ref, kp_ref, kc_ref, vp_ref, vc_ref, o_ref):
    j = pl.program_id(1)
    w = WINDOW
    qi = lax.broadcasted_iota(I32, (w, 2 * w), 0)
    km = lax.broadcasted_iota(I32, (w, 2 * w), 1)
    band = (km >= qi) & (km <= w + qi) & ((j > 0) | (km >= w))
    mask4 = jnp.concatenate([band] * GROUP, axis=0)
    lane = lax.broadcasted_iota(I32, (1, LANES), 1)
    lo = lane < HEAD_DIM
    scale = HEAD_DIM ** -0.5
    heads = range(N_KV_HEADS)

    def both_halves(prev_ref, cur_ref, h):
        pair = (h // 2) * LANES
        x = jnp.concatenate([prev_ref[:, pair:pair + LANES], cur_ref[:, pair:pair + LANES]], axis=0)
        x = jnp.where(lo if h % 2 == 0 else ~lo, x, 0.0)
        return x + pltpu.roll(x, HEAD_DIM, axis=1)

    def q_group(h):
        qs = []
        for g in range(GROUP):
            n = h * GROUP + g
            qp = q_ref[:, (n // 2) * LANES:(n // 2 + 1) * LANES]
            qs.append(jnp.where(lo if n % 2 == 0 else ~lo, qp, 0.0))
        return jnp.concatenate(qs, axis=0)

    k2 = [both_halves(kp_ref, kc_ref, h) for h in heads]
    v2 = [both_halves(vp_ref, vc_ref, h) for h in heads]
    q4 = [q_group(h) for h in heads]
    sink = [jnp.concatenate([jnp.full((w, 1), sink_ref[h * GROUP + g], F32) for g in range(GROUP)], axis=0)
            for h in heads]
    s = [jnp.where(mask4, _bdot_nt(q, k) * scale, NEG_BIG) for q, k in zip(q4, k2)]
    m = [jnp.maximum(jnp.max(x, axis=-1, keepdims=True), sk) for x, sk in zip(s, sink)]
    p = [jnp.exp(x - mm) for x, mm in zip(s, m)]
    den = [jnp.sum(x, axis=-1, keepdims=True) + jnp.exp(sk - mm) for x, sk, mm in zip(p, sink, m)]
    o4 = [_bdot(x, v) / dd for x, v, dd in zip(p, v2, den)]
    for h in heads:
        for g in range(0, GROUP, 2):
            n = h * GROUP + g
            pair_out = jnp.where(lo, o4[h][g * w:(g + 1) * w], o4[h][(g + 1) * w:(g + 2) * w])
            o_ref[:, (n // 2) * LANES:(n // 2 + 1) * LANES] = pair_out.astype(o_ref.dtype)


def swa_prompt(proj, sinks, batch, seq):
    nb = seq // WINDOW
    row = lambda b, j: b * nb + j
    return pl.pallas_call(
        _swa_prompt_kernel,
        grid=(batch, nb),
        in_specs=[
            pl.BlockSpec(memory_space=pltpu.SMEM),
            pl.BlockSpec((WINDOW, ATT_Q), lambda b, j: (row(b, j), 0)),
            pl.BlockSpec((WINDOW, ATT_KV), lambda b, j: (row(b, jnp.maximum(j - 1, 0)), ATT_Q // ATT_KV)),
            pl.BlockSpec((WINDOW, ATT_KV), lambda b, j: (row(b, j), ATT_Q // ATT_KV)),
            pl.BlockSpec((WINDOW, ATT_KV), lambda b, j: (row(b, jnp.maximum(j - 1, 0)), ATT_Q // ATT_KV + 1)),
            pl.BlockSpec((WINDOW, ATT_KV), lambda b, j: (row(b, j), ATT_Q // ATT_KV + 1)),
        ],
        out_specs=pl.BlockSpec((WINDOW, ATT_Q), lambda b, j: (row(b, j), 0)),
        out_shape=jax.ShapeDtypeStruct((batch * seq, ATT_Q), BF16),
        compiler_params=_cp(("parallel", "arbitrary")),
        name="swa_prompt",
    )(sinks, proj, proj, proj, proj, proj)


def _swa_sample_kernel(sink_ref, q_ref, kn_ref, vn_ref, ck_ref, cv_ref, o_ref, *, t_new, w_cache):
    bg = q_ref.shape[0]
    nb = bg * N_KV_HEADS
    rows = GROUP * t_new
    q = q_ref[...].reshape(nb, rows, HEAD_DIM)
    kn = kn_ref[...].reshape(nb, t_new, HEAD_DIM)
    vn = vn_ref[...].reshape(nb, t_new, HEAD_DIM)
    ck = ck_ref[...].reshape(nb, w_cache, HEAD_DIM)
    cv = cv_ref[...].reshape(nb, w_cache, HEAD_DIM)
    scale = HEAD_DIM ** -0.5
    bdims = (((2,), (2,)), ((0,), (0,)))
    s_c = lax.dot_general(q.astype(BF16), ck.astype(BF16), bdims, preferred_element_type=F32) * scale
    s_n = lax.dot_general(q.astype(BF16), kn.astype(BF16), bdims, preferred_element_type=F32) * scale
    qi_c = lax.broadcasted_iota(I32, (nb, rows, w_cache), 1) % t_new
    kc = lax.broadcasted_iota(I32, (nb, rows, w_cache), 2)
    s_c = jnp.where(kc >= w_cache + qi_c - WINDOW, s_c, NEG_BIG)
    qi_n = lax.broadcasted_iota(I32, (nb, rows, t_new), 1) % t_new
    kk = lax.broadcasted_iota(I32, (nb, rows, t_new), 2)
    s_n = jnp.where(kk <= qi_n, s_n, NEG_BIG)
    bi = lax.broadcasted_iota(I32, (nb, rows, 1), 0) % N_KV_HEADS
    gi = lax.broadcasted_iota(I32, (nb, rows, 1), 1) // t_new
    head = bi * GROUP + gi
    sink = jnp.zeros((nb, rows, 1), F32)
    for n in range(N_Q_HEADS):
        sink = jnp.where(head == n, sink_ref[n], sink)
    m = jnp.maximum(jnp.maximum(jnp.max(s_c, axis=-1, keepdims=True), jnp.max(s_n, axis=-1, keepdims=True)), sink)
    p_c = jnp.exp(s_c - m)
    p_n = jnp.exp(s_n - m)
    den = jnp.sum(p_c, axis=-1, keepdims=True) + jnp.sum(p_n, axis=-1, keepdims=True) + jnp.exp(sink - m)
    pv = (((2,), (1,)), ((0,), (0,)))
    o = (lax.dot_general(p_c.astype(BF16), cv.astype(BF16), pv, preferred_element_type=F32)
         + lax.dot_general(p_n.astype(BF16), vn.astype(BF16), pv, preferred_element_type=F32)) / den
    o_ref[...] = o.reshape(bg, N_KV_HEADS, rows, HEAD_DIM).astype(o_ref.dtype)


def swa_sample(q, kn, vn, ck, cv, sinks, bg=8):
    b, _, rows, _ = q.shape
    t_new, w_cache = kn.shape[2], ck.shape[2]
    spec = lambda r: pl.BlockSpec((bg, N_KV_HEADS, r, HEAD_DIM), lambda i: (i, 0, 0, 0))
    return pl.pallas_call(
        functools.partial(_swa_sample_kernel, t_new=t_new, w_cache=w_cache),
        grid=(b // bg,),
        in_specs=[pl.BlockSpec(memory_space=pltpu.SMEM), spec(rows), spec(t_new), spec(t_new),
                  spec(w_cache), spec(w_cache)],
        out_specs=spec(rows),
        out_shape=jax.ShapeDtypeStruct(q.shape, F32),
        compiler_params=_cp(("parallel",)),
        name="swa_sample",
    )(sinks, q, kn, vn, ck, cv)


def _unit_lower_inverses(a_list):
    c = a_list[0].shape[0]
    ri = lax.broadcasted_iota(I32, (c, c), 0)
    ci = lax.broadcasted_iota(I32, (c, c), 1)
    eye = jnp.where(ri == ci, 1.0, 0.0)
    ps = [-a for a in a_list]
    ts = [eye + p for p in ps]
    span = 2
    while span < c:
        ps = [_bdot(p, p) for p in ps]
        ts = [t + _bdot(t, p) for t, p in zip(ts, ps)]
        span *= 2
    return ts


def _conv_silu(ext_ref, x_ref, cw_ref, tb):
    ext_ref[SUBLANES:tb + SUBLANES, :] = x_ref[...]
    y = None
    for jj in range(CONV_W):
        term = ext_ref[SUBLANES - (CONV_W - 1) + jj:SUBLANES - (CONV_W - 1) + jj + tb, :] * cw_ref[jj:jj + 1, :]
        y = term if y is None else y + term
    ext_ref[0:SUBLANES, :] = ext_ref[tb:tb + SUBLANES, :]
    return y * _sigmoid(y)


def _l2norm(x):
    return x * lax.rsqrt(jnp.sum(x * x, axis=-1, keepdims=True) + EPS)


def _pad_a_lanes(v):
    return jnp.zeros((1, LANES), F32).at[0, N_DN_HEADS:2 * N_DN_HEADS].set(v)


def _gdn_prompt_kernel(q_ref, k_ref, v_ref, z_ref, ba_ref, cwq_ref, cwk_ref, cwv_ref, alog_ref, dtb_ref,
                       dnw_ref, o_ref, sout_ref, extq, extk, extv, s_scr, *, tb):
    hg = pl.program_id(1)
    t = pl.program_id(2)
    c = GDN_CHUNK
    nch = tb // c

    @pl.when(t == 0)
    def _():
        zero = jnp.zeros((SUBLANES, extq.shape[1]), F32)
        extq[0:SUBLANES, :] = zero
        extk[0:SUBLANES, :] = zero
        extv[0:SUBLANES, :] = zero
        s_scr[...] = jnp.zeros(s_scr.shape, F32)

    qc = _conv_silu(extq, q_ref, cwq_ref, tb)
    kc = _conv_silu(extk, k_ref, cwk_ref, tb)
    vc = _conv_silu(extv, v_ref, cwv_ref, tb)
    ba = ba_ref[...]
    lane = lax.broadcasted_iota(I32, (tb, LANES), 1)
    beta_all = _sigmoid(ba)
    g_all = -jnp.exp(alog_ref[...]) * _softplus(ba + dtb_ref[...])
    rt = lax.broadcasted_iota(I32, (tb, tb), 0)
    ct = lax.broadcasted_iota(I32, (tb, tb), 1)
    shift = c.bit_length() - 1
    same = jnp.right_shift(rt, shift) == jnp.right_shift(ct, shift)
    gcum_all = _fdot(jnp.where(same & (rt >= ct), 1.0, 0.0), g_all)
    ri = lax.broadcasted_iota(I32, (c, c), 0)
    ci = lax.broadcasted_iota(I32, (c, c), 1)
    incl = ri >= ci
    strict = ri > ci

    units = [(hh, ch) for hh in range(GDN_HG) for ch in range(nch)]
    qn_h, kn_h, v_h, beta_h, gc_h, grow_h = [], [], [], [], [], []
    for hh in range(GDN_HG):
        h = hg * GDN_HG + hh
        sl = slice(hh * DK, (hh + 1) * DK)
        qn_h.append(_l2norm(qc[:, sl]) * (DK ** -0.5))
        kn_h.append(_l2norm(kc[:, sl]))
        v_h.append(vc[:, sl])
        beta_h.append(jnp.sum(jnp.where(lane == h, beta_all, 0.0), axis=-1, keepdims=True))
        gcol = jnp.sum(jnp.where(lane == N_DN_HEADS + h, gcum_all, 0.0), axis=-1, keepdims=True)
        gb = jnp.broadcast_to(gcol, (tb, LANES))
        gc_h.append(gb)
        grow_h.append(gb.T)
    rows = lambda ch: slice(ch * c, (ch + 1) * c)
    qn = [qn_h[hh][rows(ch)] for hh, ch in units]
    kn = [kn_h[hh][rows(ch)] for hh, ch in units]
    vv = [v_h[hh][rows(ch)] for hh, ch in units]
    beta = [beta_h[hh][rows(ch)] for hh, ch in units]
    gc = [gc_h[hh][rows(ch)] for hh, ch in units]
    grow = [grow_h[hh][0:c, ch * c:(ch + 1) * c] for hh, ch in units]
    eg = [jnp.exp(g) for g in gc]
    kb = [k * b for k, b in zip(kn, beta)]
    vb = [v * b for v, b in zip(vv, beta)]
    kexp = [k * e for k, e in zip(kb, eg)]
    qexp = [q * e for q, e in zip(qn, eg)]
    glast = [g[c - 1:c, :] for g in gc]
    kend = [k * jnp.exp(gl - g) for k, gl, g in zip(kn, glast, gc)]
    sdec = [jnp.exp(gl) for gl in glast]
    ak = [_bdot_nt(jnp.concatenate([a, q], axis=0), k) for a, q, k in zip(kb, qn, kn)]
    decay = [jnp.where(incl, jnp.exp(jnp.where(incl, g[:, :c] - gr, 0.0)), 0.0) for g, gr in zip(gc, grow)]
    a_mat = [jnp.where(strict, x[:c] * d, 0.0) for x, d in zip(ak, decay)]
    qk = [jnp.where(incl, x[c:] * d, 0.0) for x, d in zip(ak, decay)]
    t_inv = _unit_lower_inverses(a_mat)
    uw = [_bdot(ti, jnp.concatenate([v, k], axis=1)) for ti, v, k in zip(t_inv, vb, kexp)]

    s = [s_scr[hh] for hh in range(GDN_HG)]
    outs = [[] for _ in range(GDN_HG)]
    for ch in range(nch):
        idx = [hh * nch + ch for hh in range(GDN_HG)]
        ws = [_bdot(jnp.concatenate([uw[i][:, DV:], qexp[i]], axis=0), s[hh]) for hh, i in enumerate(idx)]
        u = [uw[i][:, :DV] - w[:c] for i, w in zip(idx, ws)]
        o = [w[c:] + _bdot(qk[i], uu) for i, w, uu in zip(idx, ws, u)]
        s = [s[hh] * sdec[i] + _bdot_tn(kend[i], uu) for (hh, i), uu in zip(enumerate(idx), u)]
        for hh in range(GDN_HG):
            outs[hh].append(o[hh])
    dnw = dnw_ref[...]
    for hh in range(GDN_HG):
        s_scr[hh] = s[hh]
        sl = slice(hh * DK, (hh + 1) * DK)
        o = jnp.concatenate(outs[hh], axis=0)
        o = o * lax.rsqrt(jnp.mean(o * o, axis=-1, keepdims=True) + EPS) * dnw
        zh = z_ref[:, sl]
        o_ref[:, sl] = (o * (zh * _sigmoid(zh))).astype(o_ref.dtype)

    @pl.when(t == pl.num_programs(2) - 1)
    def _():
        sout_ref[0] = s_scr[...]


def gdn_prompt(proj, ba, conv_w, a_log, dt_bias, dn_norm, batch, seq):
    tb = GDN_TBLOCK
    nt = seq // tb
    wblk = GDN_HG * DK
    q0, k0, v0, z0 = (COL_QKV_D // wblk, (COL_QKV_D + N_DN_HEADS * DK) // wblk,
                      (COL_QKV_D + 2 * N_DN_HEADS * DK) // wblk, COL_Z // wblk)
    nhg = N_DN_HEADS // GDN_HG
    row = lambda b, t: b * nt + t
    col_spec = lambda c0: pl.BlockSpec((tb, wblk), lambda b, hg, t: (row(b, t), c0 + hg))
    cw_spec = lambda c0: pl.BlockSpec((CONV_W, wblk), lambda b, hg, t: (0, c0 + hg))
    vec = pl.BlockSpec((1, LANES), lambda b, hg, t: (0, 0))
    return pl.pallas_call(
        functools.partial(_gdn_prompt_kernel, tb=tb),
        grid=(batch, nhg, nt),
        in_specs=[col_spec(q0), col_spec(k0), col_spec(v0), col_spec(z0),
                  pl.BlockSpec((tb, LANES), lambda b, hg, t: (row(b, t), 0)),
                  cw_spec(0), cw_spec(nhg), cw_spec(2 * nhg), vec, vec, vec],
        out_specs=[pl.BlockSpec((tb, wblk), lambda b, hg, t: (row(b, t), hg)),
                   pl.BlockSpec((1, GDN_HG, DK, DV), lambda b, hg, t: (b, hg, 0, 0))],
        out_shape=[jax.ShapeDtypeStruct((batch * seq, DN_V), BF16),
                   jax.ShapeDtypeStruct((batch, N_DN_HEADS, DK, DV), F32)],
        scratch_shapes=[pltpu.VMEM((tb + SUBLANES, wblk), F32)] * 3 + [pltpu.VMEM((GDN_HG, DK, DV), F32)],
        compiler_params=_cp(("parallel", "parallel", "arbitrary")),
        name="gdn_prompt",
    )(proj, proj, proj, proj, ba, conv_w, conv_w, conv_w, _pad_a_lanes(a_log), _pad_a_lanes(dt_bias),
      dn_norm.reshape(1, DV))


def _gdn_sample_kernel(xp_ref, z_ref, ba_ref, cw_ref, alog_ref, dtb_ref, dnw_ref, s0_ref, o_ref, s_ref, *, t_new):
    bg = xp_ref.shape[0]
    rows = xp_ref.shape[1]
    ri = lax.broadcasted_iota(I32, (rows, LANES), 0)
    vmask = ri < t_new
    r8 = lax.broadcasted_iota(I32, (rows, rows), 0)
    c8 = lax.broadcasted_iota(I32, (rows, rows), 1)
    incl = r8 >= c8
    strict = r8 > c8
    ltri = jnp.where(incl, 1.0, 0.0)
    ones8 = jnp.ones((rows, rows), F32)
    dnw = dnw_ref[...]

    def per_batch(b, carry):
        xp = xp_ref[b]
        y = None
        for jj in range(CONV_W):
            sh = xp if jj == 0 else pltpu.roll(xp, rows - jj, axis=0)
            term = sh * cw_ref[jj:jj + 1, :]
            y = term if y is None else y + term
        y = y * _sigmoid(y)
        ba = ba_ref[b]
        beta_all = jnp.where(vmask, _sigmoid(ba), 0.0)
        g_all = jnp.where(vmask, -jnp.exp(alog_ref[...]) * _softplus(ba + dtb_ref[...]), 0.0)
        gcum_all = _fdot(ltri, g_all)
        z = z_ref[b]
        for h0 in range(0, N_DN_HEADS, GDN_SAMPLE_HEADS):
            hs = list(range(h0, h0 + GDN_SAMPLE_HEADS))
            qn = [_l2norm(y[:, h * DK:(h + 1) * DK]) * (DK ** -0.5) for h in hs]
            kn = [jnp.where(vmask, _l2norm(y[:, (N_DN_HEADS + h) * DK:(N_DN_HEADS + h + 1) * DK]), 0.0)
                  for h in hs]
            vh = [y[:, (2 * N_DN_HEADS + h) * DK:(2 * N_DN_HEADS + h + 1) * DK] for h in hs]
            beta = [beta_all[:, h:h + 1] for h in hs]
            gc = [jnp.broadcast_to(gcum_all[:, N_DN_HEADS + h:N_DN_HEADS + h + 1], (rows, LANES)) for h in hs]
            grow = [_fdot(ones8, jnp.where(r8 == c8, g[:, :rows], 0.0)) for g in gc]
            decay = [jnp.where(incl, jnp.exp(jnp.where(incl, g[:, :rows] - gr, 0.0)), 0.0)
                     for g, gr in zip(gc, grow)]
            eg = [jnp.exp(g) for g in gc]
            kb = [k * bt for k, bt in zip(kn, beta)]
            vb = [v * bt for v, bt in zip(vh, beta)]
            kexp = [k * e for k, e in zip(kb, eg)]
            qexp = [q * e for q, e in zip(qn, eg)]
            glast = [g[rows - 1:rows, :] for g in gc]
            kend = [k * jnp.exp(gl - g) for k, gl, g in zip(kn, glast, gc)]
            sdec = [jnp.exp(gl) for gl in glast]
            ak = [_bdot_nt(jnp.concatenate([a, q], axis=0), k) for a, q, k in zip(kb, qn, kn)]
            a_mat = [jnp.where(strict, x[:rows] * d, 0.0) for x, d in zip(ak, decay)]
            qk = [jnp.where(incl, x[rows:] * d, 0.0) for x, d in zip(ak, decay)]
            t_inv = _unit_lower_inverses(a_mat)
            uw = [_bdot(ti, jnp.concatenate([v, k], axis=1)) for ti, v, k in zip(t_inv, vb, kexp)]
            s = [s0_ref[b, h] for h in hs]
            ws = [_bdot(jnp.concatenate([x[:, DV:], q], axis=0), a) for x, q, a in zip(uw, qexp, s)]
            u = [x[:, :DV] - w[:rows] for x, w in zip(uw, ws)]
            o_rows = [w[rows:] + _bdot(m, uu) for w, m, uu in zip(ws, qk, u)]
            s = [a * sd + _bdot_tn(k, uu) for a, sd, k, uu in zip(s, sdec, kend, u)]
            for h, a, orow in zip(hs, s, o_rows):
                s_ref[b, h] = a
                o = orow * lax.rsqrt(jnp.mean(orow * orow, axis=-1, keepdims=True) + EPS) * dnw
                zh = z[:, h * DV:(h + 1) * DV]
                o_ref[b, :, h * DV:(h + 1) * DV] = (o * (zh * _sigmoid(zh))).astype(o_ref.dtype)
        return carry

    lax.fori_loop(0, bg, per_batch, 0)


def gdn_sample(xp, z, ba, conv_w, a_log, dt_bias, dn_norm, s0, t_new, bg=4):
    b = xp.shape[0]
    rows = xp.shape[1]
    full = lambda shape: pl.BlockSpec(shape, lambda i: (0,) * len(shape))
    return pl.pallas_call(
        functools.partial(_gdn_sample_kernel, t_new=t_new),
        grid=(b // bg,),
        in_specs=[pl.BlockSpec((bg, rows, CONV_DIM), lambda i: (i, 0, 0)),
                  pl.BlockSpec((bg, rows, DN_V), lambda i: (i, 0, 0)),
                  pl.BlockSpec((bg, rows, LANES), lambda i: (i, 0, 0)),
                  full((CONV_W, CONV_DIM)), full((1, LANES)), full((1, LANES)), full((1, DV)),
                  pl.BlockSpec((bg, N_DN_HEADS, DK, DV), lambda i: (i, 0, 0, 0))],
        out_specs=[pl.BlockSpec((bg, rows, DN_V), lambda i: (i, 0, 0)),
                   pl.BlockSpec((bg, N_DN_HEADS, DK, DV), lambda i: (i, 0, 0, 0))],
        out_shape=[jax.ShapeDtypeStruct((b, rows, DN_V), BF16),
                   jax.ShapeDtypeStruct((b, N_DN_HEADS, DK, DV), F32)],
        compiler_params=_cp(("parallel",)),
        name="gdn_sample",
    )(xp, z, ba, conv_w, _pad_a_lanes(a_log), _pad_a_lanes(dt_bias), dn_norm.reshape(1, DV), s0)


def _router_kernel(x_ref, g_ref, wr_ref, br_ref, mi_ref, mf_ref, cnt_ref, run_ref):
    i = pl.program_id(0)
    tm = x_ref.shape[0]

    @pl.when(i == 0)
    def _():
        run_ref[...] = jnp.zeros(run_ref.shape, F32)

    x = x_ref[...]
    xn = x * lax.rsqrt(jnp.mean(x * x, axis=-1, keepdims=True) + EPS) * g_ref[...]
    lane = lax.broadcasted_iota(I32, (tm, LANES), 1)
    lane_f = lane.astype(F32)
    logits = _fdot(xn, wr_ref[...]) + br_ref[...]
    work = jnp.where(lane < N_EXPERTS, logits, -jnp.inf)
    vals, idxs, hots = [], [], []
    for _ in range(TOP_K):
        mx = jnp.max(work, axis=-1, keepdims=True)
        idx_f = jnp.min(jnp.where(work == mx, lane_f, float(LANES)), axis=-1, keepdims=True)
        idx = idx_f.astype(I32)
        hot = lane == idx
        work = jnp.where(hot, -jnp.inf, work)
        vals.append(mx)
        idxs.append(idx)
        hots.append(hot)
    exps = [jnp.exp(v - vals[0]) for v in vals]
    den = exps[0] + exps[1] + exps[2] + exps[3]
    member = jnp.where(hots[0] | hots[1] | hots[2] | hots[3], 1.0, 0.0)
    ri = lax.broadcasted_iota(I32, (tm, tm), 0)
    ci = lax.broadcasted_iota(I32, (tm, tm), 1)
    before = _bdot(jnp.where(ri > ci, 1.0, 0.0), member) + run_ref[0:1, :]
    run_ref[...] = run_ref[...] + jnp.sum(member, axis=0, keepdims=True)
    mi = jnp.zeros((tm, LANES), I32)
    mf = jnp.zeros((tm, LANES), F32)
    for k in range(TOP_K):
        rank = jnp.sum(jnp.where(hots[k], before, 0.0), axis=-1, keepdims=True).astype(I32)
        mi = jnp.where(lane == k, idxs[k], mi)
        mi = jnp.where(lane == TOP_K + k, rank, mi)
        mf = jnp.where(lane == k, exps[k] / den, mf)
    mi_ref[...] = mi
    mf_ref[...] = mf
    cnt_ref[...] = run_ref[...]


def router(x, gamma, w_router, b_router):
    m, d = x.shape
    tm = TOK_TILE
    wr = jnp.zeros((d, LANES), F32).at[:, :N_EXPERTS].set(w_router)
    br = jnp.zeros((1, LANES), F32).at[0, :N_EXPERTS].set(b_router)
    return pl.pallas_call(
        _router_kernel,
        grid=(m // tm,),
        in_specs=[pl.BlockSpec((tm, d), lambda i: (i, 0)), pl.BlockSpec((1, d), lambda i: (0, 0)),
                  pl.BlockSpec((d, LANES), lambda i: (0, 0)), pl.BlockSpec((1, LANES), lambda i: (0, 0))],
        out_specs=[pl.BlockSpec((tm, LANES), lambda i: (i, 0)), pl.BlockSpec((tm, LANES), lambda i: (i, 0)),
                   pl.BlockSpec((SUBLANES, LANES), lambda i: (0, 0))],
        out_shape=[jax.ShapeDtypeStruct((m, LANES), I32), jax.ShapeDtypeStruct((m, LANES), F32),
                   jax.ShapeDtypeStruct((SUBLANES, LANES), F32)],
        scratch_shapes=[pltpu.VMEM((SUBLANES, LANES), F32)],
        compiler_params=_cp(("arbitrary",)),
        name="router",
    )(x, gamma.reshape(1, d), wr, br)


def _zero_kernel(sub_ref, o_ref):
    del sub_ref
    o_ref[...] = jnp.zeros(o_ref.shape, o_ref.dtype)


def zero_partial_subblocks(sub_idx, n_rows, d):
    return pl.pallas_call(
        _zero_kernel,
        grid_spec=pltpu.PrefetchScalarGridSpec(
            num_scalar_prefetch=1, grid=(sub_idx.shape[0],), in_specs=[],
            out_specs=pl.BlockSpec((MOE_SUB, d), lambda e, sub: (sub[e], 0))),
        out_shape=jax.ShapeDtypeStruct((n_rows, d), F32),
        compiler_params=_cp(("arbitrary",)),
        name="moe_zero_tails",
    )(sub_idx)


def _row_copy(src_ref, src_row, dst_ref, dst_row, sem):
    return pltpu.make_async_copy(src_ref.at[pl.ds(src_row, 1), :], dst_ref.at[pl.ds(dst_row, 1), :], sem)


def _load_dest(dest_ref, dest_smem, sem):
    cp = pltpu.make_async_copy(dest_ref, dest_smem, sem)
    cp.start()
    cp.wait()


def _dest_of(dest_smem, r, k):
    per_row = LANES // TOP_K
    return dest_smem[r // per_row, (r % per_row) * TOP_K + k]


def _dispatch_kernel(dest_ref, x_ref, xs_in_ref, xs_ref, dest_smem, sem_idx, sem):
    del xs_in_ref
    tm = x_ref.shape[0]
    _load_dest(dest_ref, dest_smem, sem_idx)

    def issue(r, c):
        for k in range(TOP_K):
            _row_copy(x_ref, r, xs_ref, _dest_of(dest_smem, r, k), sem).start(priority=k % 2)
        return c

    lax.fori_loop(0, tm, issue, 0)

    def drain(r, c):
        for k in range(TOP_K):
            _row_copy(x_ref, 0, xs_ref, 0, sem).wait()
        return c

    lax.fori_loop(0, tm, drain, 0)


def dispatch(x, dest2d, xs):
    m, d = x.shape
    tm = TOK_TILE
    drows = tm * TOP_K // LANES
    return pl.pallas_call(
        _dispatch_kernel,
        grid=(m // tm,),
        in_specs=[pl.BlockSpec((drows, LANES), lambda i: (i, 0)),
                  pl.BlockSpec((tm, d), lambda i: (i, 0)),
                  pl.BlockSpec(memory_space=pl.ANY)],
        out_specs=pl.BlockSpec(memory_space=pl.ANY),
        out_shape=jax.ShapeDtypeStruct(xs.shape, xs.dtype),
        scratch_shapes=[pltpu.SMEM((drows, LANES), I32), pltpu.SemaphoreType.DMA, pltpu.SemaphoreType.DMA],
        input_output_aliases={2: 0},
        compiler_params=_cp(("arbitrary",)),
        name="moe_dispatch",
    )(dest2d, x, xs)


def _expert_kernel(blk_ref, bexp_ref, nsub_ref, last_ref, x0_ref, x1_ref, x2_ref, x3_ref, g_ref, wg_ref, wu_ref,
                   bg_ref, bu_ref, wo_ref, bo_ref, o_ref, xn_ref, wgb_ref, wub_ref, wob_ref):
    del blk_ref, bexp_ref, last_ref
    b = pl.program_id(0)
    f = pl.program_id(1)
    n = nsub_ref[b]
    x_refs = (x0_ref, x1_ref, x2_ref, x3_ref)

    @pl.when(f == 0)
    def _():
        for s, x_ref in enumerate(x_refs):
            rows = slice(s * MOE_SUB, (s + 1) * MOE_SUB)

            @pl.when(s < n)
            def _():
                x = x_ref[...]
                xn = x * lax.rsqrt(jnp.mean(x * x, axis=-1, keepdims=True) + EPS) * g_ref[...]
                xn_ref[rows, :] = xn.astype(BF16)
                o_ref[rows, :] = jnp.broadcast_to(bo_ref[...], (MOE_SUB, o_ref.shape[1]))

            @pl.when((n > 0) & (s >= n))
            def _():
                o_ref[rows, :] = jnp.zeros((MOE_SUB, o_ref.shape[1]), F32)

    @pl.when(n > 0)
    def _():
        wgb_ref[...] = wg_ref[...].astype(BF16)
        wub_ref[...] = wu_ref[...].astype(BF16)
        wob_ref[...] = wo_ref[...].astype(BF16)

    def run(subs):
        rows = [pl.ds(pl.multiple_of(s * MOE_SUB, MOE_SUB), MOE_SUB) for s in subs]
        xs = [xn_ref[r, :] for r in rows]
        gate = [jnp.dot(x, wgb_ref[...], preferred_element_type=F32) + bg_ref[...] for x in xs]
        up = [jnp.dot(x, wub_ref[...], preferred_element_type=F32) + bu_ref[...] for x in xs]
        gate = [jnp.minimum(g, SWIGLU_LIMIT) for g in gate]
        up = [jnp.clip(u, -SWIGLU_LIMIT, SWIGLU_LIMIT) for u in up]
        act = [((u + 1.0) * g * _sigmoid(SWIGLU_ALPHA * g)).astype(BF16) for u, g in zip(up, gate)]
        for r, a in zip(rows, act):
            o_ref[r, :] += jnp.dot(a, wob_ref[...], preferred_element_type=F32)

    def pair(p, c):
        run([2 * p, 2 * p + 1])
        return c

    lax.fori_loop(0, lax.shift_right_logical(n, 1), pair, 0)

    @pl.when(lax.bitwise_and(n, 1) == 1)
    def _():
        run([n - 1])


def experts(xs, gamma, w_in, b_in, w_out, b_out, blk, bexp, nsub, last_sub):
    n_rows, d = xs.shape
    nb = blk.shape[0]
    nf = D_FF // MOE_TF
    per = MOE_ROWS // MOE_SUB
    assert per == 4

    def x_spec(s):
        return pl.BlockSpec((MOE_SUB, d),
                            lambda b, f, blk, bexp, nsub, last: (jnp.minimum(per * blk[b] + s, last[b]), 0))

    wsel = lambda b, f, blk, bexp, nsub, last: bexp[b]
    in_specs = [x_spec(0), x_spec(1), x_spec(2), x_spec(3),
                pl.BlockSpec((1, d), lambda b, f, *_: (0, 0)),
                pl.BlockSpec((None, d, MOE_TF), lambda b, f, *p: (wsel(b, f, *p), 0, f)),
                pl.BlockSpec((None, d, MOE_TF), lambda b, f, *p: (wsel(b, f, *p), 0, nf + f)),
                pl.BlockSpec((None, 1, MOE_TF), lambda b, f, *p: (wsel(b, f, *p), 0, f)),
                pl.BlockSpec((None, 1, MOE_TF), lambda b, f, *p: (wsel(b, f, *p), 0, nf + f)),
                pl.BlockSpec((None, MOE_TF, d), lambda b, f, *p: (wsel(b, f, *p), f, 0)),
                pl.BlockSpec((None, 1, d), lambda b, f, *p: (wsel(b, f, *p), 0, 0))]
    return pl.pallas_call(
        _expert_kernel,
        grid_spec=pltpu.PrefetchScalarGridSpec(
            num_scalar_prefetch=4, grid=(nb, nf), in_specs=in_specs,
            out_specs=pl.BlockSpec((MOE_ROWS, d), lambda b, f, blk, bexp, nsub, last: (blk[b], 0)),
            scratch_shapes=[pltpu.VMEM((MOE_ROWS, d), BF16), pltpu.VMEM((d, MOE_TF), BF16),
                            pltpu.VMEM((d, MOE_TF), BF16), pltpu.VMEM((MOE_TF, d), BF16)]),
        out_shape=jax.ShapeDtypeStruct((n_rows, d), F32),
        compiler_params=_cp(("arbitrary", "arbitrary")),
        name="moe_experts",
    )(blk, bexp, nsub, last_sub, xs, xs, xs, xs, gamma.reshape(1, d), w_in, w_in,
      b_in.reshape(N_EXPERTS, 1, 2 * D_FF), b_in.reshape(N_EXPERTS, 1, 2 * D_FF), w_out,
      b_out.reshape(N_EXPERTS, 1, d))


def _combine_kernel(dest_ref, x_ref, mf_ref, ys_ref, o_ref, buf, dest_smem, sem_idx, sem):
    tm = x_ref.shape[0]
    _load_dest(dest_ref, dest_smem, sem_idx)

    def issue(r, c):
        for k in range(TOP_K):
            _row_copy(ys_ref, _dest_of(dest_smem, r, k), buf.at[k], r, sem).start(priority=k % 2)
        return c

    lax.fori_loop(0, tm, issue, 0)

    def drain(r, c):
        for k in range(TOP_K):
            _row_copy(ys_ref, 0, buf.at[k], 0, sem).wait()
        return c

    lax.fori_loop(0, tm, drain, 0)
    gates = mf_ref[...]
    acc = x_ref[...]
    for k in range(TOP_K):
        acc = acc + gates[:, k:k + 1] * buf[k]
    o_ref[...] = acc


def combine(x, mf, dest2d, ys):
    m, d = x.shape
    tm = TOK_TILE
    drows = tm * TOP_K // LANES
    return pl.pallas_call(
        _combine_kernel,
        grid=(m // tm,),
        in_specs=[pl.BlockSpec((drows, LANES), lambda i: (i, 0)),
                  pl.BlockSpec((tm, d), lambda i: (i, 0)),
                  pl.BlockSpec((tm, LANES), lambda i: (i, 0)),
                  pl.BlockSpec(memory_space=pl.ANY)],
        out_specs=pl.BlockSpec((tm, d), lambda i: (i, 0)),
        out_shape=jax.ShapeDtypeStruct((m, d), F32),
        scratch_shapes=[pltpu.VMEM((TOP_K, tm, d), F32), pltpu.SMEM((drows, LANES), I32),
                        pltpu.SemaphoreType.DMA, pltpu.SemaphoreType.DMA],
        compiler_params=_cp(("arbitrary",)),
        name="moe_combine",
    )(dest2d, x, mf, ys)


def routed_experts_residual(x, gamma, w_router, b_router, w_exp_in, b_exp_in, w_exp_out, b_exp_out):
    m, d = x.shape
    mi, mf, cnt = router(x, gamma, w_router, b_router)
    idx = mi[:, :TOP_K]
    rank = mi[:, TOP_K:2 * TOP_K]
    counts = cnt[0, :N_EXPERTS].astype(I32)
    padded = (counts + MOE_ROWS - 1) // MOE_ROWS * MOE_ROWS
    pad_end = jnp.cumsum(padded)
    pad_start = pad_end - padded
    onehot = idx[:, :, None] == jnp.arange(N_EXPERTS, dtype=I32)[None, None, :]
    dest = jnp.sum(jnp.where(onehot, pad_start[None, None, :], 0), axis=-1) + rank
    dest2d = dest.reshape(m * TOP_K // LANES, LANES)
    n_blocks = (m * TOP_K + N_EXPERTS * (MOE_ROWS - 1) + MOE_ROWS - 1) // MOE_ROWS
    used = pad_end[-1] // MOE_ROWS
    bids = jnp.arange(n_blocks, dtype=I32)
    blk = jnp.minimum(bids, jnp.maximum(used - 1, 0))
    bexp = jnp.minimum(jnp.sum((pad_end[None, :] <= (blk * MOE_ROWS)[:, None]).astype(I32), axis=1), N_EXPERTS - 1)
    fill_end = pad_start + counts
    valid_rows = jnp.clip(fill_end[bexp] - blk * MOE_ROWS, 0, MOE_ROWS)
    nsub = jnp.where(bids < used, (valid_rows + MOE_SUB - 1) // MOE_SUB, 0).astype(I32)
    per = MOE_ROWS // MOE_SUB
    last_sub = (per * blk + jnp.maximum(nsub, 1) - 1).astype(I32)
    n_rows = n_blocks * MOE_ROWS
    tail_sub = jnp.minimum(fill_end // MOE_SUB, n_rows // MOE_SUB - 1).astype(I32)
    xs = zero_partial_subblocks(tail_sub, n_rows, d)
    xs = dispatch(x, dest2d, xs)
    ys = experts(xs, gamma, w_exp_in, b_exp_in, w_exp_out, b_exp_out, blk, bexp, nsub, last_sub)
    return combine(x, mf, dest2d, ys)


def _ep_plain(dots, extras):
    return dots[0]


def _ep_sigmoid(dots, extras):
    return _sigmoid(dots[0])


def _ep_merge(dots, extras):
    return extras[0] * dots[0] + extras[1] * dots[1]


def _ep_residual(dots, extras):
    return extras[0] + dots[0]


def _ep_ple(dots, extras):
    return extras[0] + _sigmoid(dots[0]) * dots[1]


def kernel(x_prompt, x_sample, cache_k, cache_v, state_conv, state_ssm, p_prompt, p_sample, w_in, sinks, conv_w, a_log, dt_bias, dn_norm, w_attn_branch, w_dn_branch, w_out, w_router, b_router, w_exp_in, b_exp_in, w_exp_out, b_exp_out, w_ple_proj, w_ple_gate, norm_mix, norm_moe, norm_ple, norm_final):
    depth = w_in.shape[0]
    bp, seq, d = x_prompt.shape
    bs, t_new, _ = x_sample.shape
    w_cache = cache_k.shape[2]
    mp, ms = bp * seq, bs * t_new
    x = (x_prompt.reshape(mp, d), x_sample.reshape(ms, d))
    p_prompt, p_sample = p_prompt.reshape(depth, mp, -1), p_sample.reshape(depth, ms, -1)
    m = mp + ms
    tm_split = 512
    assert mp % tm_split == 0 and ms % tm_split == 0
    tm_big = 1536 if m % 1536 == 0 else 512
    tm_mid = 768 if m % 768 == 0 else 512
    outs = {k: [] for k in ("kp", "vp", "cp", "sp", "ks", "vs", "cs", "ss")}
    for i in range(depth):
        h0 = rms_rows(x, norm_mix[i], BF16, tm_split)
        proj = fused_mm([(h0, w_in[i], 0)], [], _ep_plain, COL_BA, F32, tm_big, 512, "proj_main")
        w_ba = jnp.zeros((d, LANES), F32).at[:, :2 * N_DN_HEADS].set(w_in[i][:, COL_BA:COL_GATES])
        ba = fused_mm([(h0, w_ba, 0)], [], _ep_plain, LANES, F32, tm_big, LANES, "proj_ba")
        gates = fused_mm([(h0, w_in[i][:, COL_GATES:], 0)], [], _ep_sigmoid, 2 * d, F32, tm_big, 512, "proj_gates")

        o_a_p = swa_prompt(proj, sinks[i], bp, seq)
        ps = proj[mp:]
        q_s = ps[:, :ATT_Q].reshape(bs, t_new, N_KV_HEADS, GROUP, HEAD_DIM).transpose(0, 2, 3, 1, 4)
        q_s = q_s.reshape(bs, N_KV_HEADS, GROUP * t_new, HEAD_DIM)
        k_s = ps[:, ATT_Q:ATT_Q + ATT_KV].reshape(bs, t_new, N_KV_HEADS, HEAD_DIM)
        v_s = ps[:, ATT_Q + ATT_KV:COL_QKV_D].reshape(bs, t_new, N_KV_HEADS, HEAD_DIM)
        o_a_s = swa_sample(q_s, k_s.transpose(0, 2, 1, 3), v_s.transpose(0, 2, 1, 3),
                           cache_k[i].transpose(0, 2, 1, 3), cache_v[i].transpose(0, 2, 1, 3), sinks[i])
        o_a_s = o_a_s.reshape(bs, N_KV_HEADS, GROUP, t_new, HEAD_DIM).transpose(0, 3, 1, 2, 4).reshape(ms, ATT_Q)
        o_a = jnp.concatenate([o_a_p, o_a_s.astype(BF16)], axis=0)
        tail = lambda n, c0, c1: jnp.stack(
            [lax.slice(proj, ((b + 1) * seq - n, c0), ((b + 1) * seq, c1)) for b in range(bp)])
        outs["kp"].append(tail(WINDOW, ATT_Q, ATT_Q + ATT_KV).reshape(bp, WINDOW, N_KV_HEADS, HEAD_DIM))
        outs["vp"].append(tail(WINDOW, ATT_Q + ATT_KV, COL_QKV_D).reshape(bp, WINDOW, N_KV_HEADS, HEAD_DIM))
        outs["ks"].append(jnp.concatenate([cache_k[i], k_s], axis=1)[:, -w_cache:])
        outs["vs"].append(jnp.concatenate([cache_v[i], v_s], axis=1)[:, -w_cache:])

        o_d_p, s_p = gdn_prompt(proj, ba, conv_w[i], a_log[i], dt_bias[i], dn_norm[i], bp, seq)
        qkv_s = ps[:, COL_QKV_D:COL_Z].reshape(bs, t_new, CONV_DIM)
        rows = SUBLANES
        pad_rows = rows - (CONV_W - 1) - t_new
        xp_s = jnp.concatenate([state_conv[i], qkv_s, jnp.zeros((bs, pad_rows, CONV_DIM), F32)], axis=1)
        tok_pad = lambda a: jnp.pad(a.reshape(bs, t_new, -1), ((0, 0), (0, rows - t_new), (0, 0)))
        o_d_s, s_s = gdn_sample(xp_s, tok_pad(ps[:, COL_Z:COL_BA]), tok_pad(ba[mp:]), conv_w[i], a_log[i],
                                dt_bias[i], dn_norm[i], state_ssm[i], t_new)
        o_d = jnp.concatenate([o_d_p, o_d_s[:, :t_new].reshape(ms, DN_V)], axis=0)
        outs["cp"].append(tail(CONV_W - 1, COL_QKV_D, COL_Z))
        outs["cs"].append(jnp.concatenate([state_conv[i], qkv_s], axis=1)[:, -(CONV_W - 1):])
        outs["sp"].append(s_p)
        outs["ss"].append(s_s)

        merged = _merge_call(o_a, o_d, w_attn_branch[i], w_dn_branch[i], gates, tm_mid)
        x = fused_mm([(merged, w_out[i], 0)], [x], _ep_residual, d, F32, tm_split, 512, "out_proj")

        x = routed_experts_residual(x, norm_moe[i], w_router[i], b_router[i], w_exp_in[i], b_exp_in[i],
                                    w_exp_out[i], b_exp_out[i])

        hp = rms_rows(x, norm_ple[i], BF16)
        x = fused_mm([(hp, w_ple_gate[i], 0), ((p_prompt[i], p_sample[i]), w_ple_proj[i], 0)], [x], _ep_ple, d, F32,
                     tm_split, 512, "ple")
    y_p, y_s = rms_rows(x, norm_final, F32, tm_split, split_rows=(mp, ms))
    st = lambda k: jnp.stack(outs[k])
    return (y_p.reshape(bp, seq, d), y_s.reshape(bs, t_new, d), st("kp"), st("vp"), st("cp"), st("sp"),
            st("ks"), st("vs"), st("cs"), st("ss"))


def _merge_call(o_a, o_d, w_a, w_b, gates, tm):
    m = o_a.shape[0]
    d = w_a.shape[1]
    tn = 512
    nj = d // tn
    ka, kb = o_a.shape[1], o_d.shape[1]

    def body(a_ref, b_ref, wa_ref, wb_ref, ga_ref, gb_ref, o_ref, wab_ref, wbb_ref):
        @pl.when(pl.program_id(1) == 0)
        def _():
            wab_ref[...] = wa_ref[...].astype(BF16)
            wbb_ref[...] = wb_ref[...].astype(BF16)

        da = jnp.dot(a_ref[...], wab_ref[...], preferred_element_type=F32)
        db = jnp.dot(b_ref[...], wbb_ref[...], preferred_element_type=F32)
        o_ref[...] = (ga_ref[...] * da + gb_ref[...] * db).astype(o_ref.dtype)

    return pl.pallas_call(
        body,
        grid=(nj, m // tm),
        in_specs=[pl.BlockSpec((tm, ka), lambda j, i: (i, 0)), pl.BlockSpec((tm, kb), lambda j, i: (i, 0)),
                  pl.BlockSpec((ka, tn), lambda j, i: (0, j)), pl.BlockSpec((kb, tn), lambda j, i: (0, j)),
                  pl.BlockSpec((tm, tn), lambda j, i: (i, j)), pl.BlockSpec((tm, tn), lambda j, i: (i, nj + j))],
        out_specs=pl.BlockSpec((tm, tn), lambda j, i: (i, j)),
        out_shape=jax.ShapeDtypeStruct((m, d), BF16),
        scratch_shapes=[pltpu.VMEM((ka, tn), BF16), pltpu.VMEM((kb, tn), BF16)],
        compiler_params=_cp(("arbitrary", "arbitrary")),
        name="branch_merge",
    )(o_a, o_d, w_a, w_b, gates, gates)
```

```python
import collections
import functools

import jax
import jax.numpy as jnp
from jax import lax
from jax.experimental import pallas as pl
from jax.experimental.pallas import tpu as pltpu

F32 = jnp.float32
BF16 = jnp.bfloat16
I32 = jnp.int32

D_MODEL = 2048
N_Q_HEADS = 16
N_KV_HEADS = 4
GROUP = N_Q_HEADS // N_KV_HEADS
HEAD_DIM = 64
WINDOW = 128
N_DN_HEADS = 16
DK = 128
DV = 128
CONV_W = 4
CONV_DIM = N_DN_HEADS * (2 * DK + DV)
N_EXPERTS = 32
TOP_K = 4
D_FF = D_MODEL
SWIGLU_LIMIT = 7.0
SWIGLU_ALPHA = 1.702
EPS = 1e-6
ATT_Q = N_Q_HEADS * HEAD_DIM
ATT_KV = N_KV_HEADS * HEAD_DIM
DN_V = N_DN_HEADS * DV

COL_QKV_D = ATT_Q + 2 * ATT_KV
COL_Z = COL_QKV_D + CONV_DIM
COL_BA = COL_Z + DN_V
COL_GATES = COL_BA + 2 * N_DN_HEADS

LANES = 128
SUBLANES = 8
VMEM_LIMIT_BYTES = 56 * 2 ** 20

GDN_CHUNK = 64
GDN_TBLOCK = 256
GDN_HG = 4
GDN_SAMPLE_HEADS = 8
MOE_ROWS = 1024
MOE_SUB = 256
MOE_TF = 256
TOK_TILE = 256
NEG_BIG = -1e30


def _cp(sem, vmem=VMEM_LIMIT_BYTES):
    return pltpu.CompilerParams(dimension_semantics=sem, vmem_limit_bytes=vmem)


def _sigmoid(x):
    return 1.0 / (1.0 + jnp.exp(-x))


def _softplus(x):
    return jnp.maximum(x, 0.0) + jnp.log1p(jnp.exp(-jnp.abs(x)))


def _bdot(a, b):
    return jnp.dot(a.astype(BF16), b.astype(BF16), preferred_element_type=F32)


def _bdot_nt(a, b):
    return lax.dot_general(a.astype(BF16), b.astype(BF16), (((1,), (1,)), ((), ())),
                           preferred_element_type=F32)


def _bdot_tn(a, b):
    return lax.dot_general(a.astype(BF16), b.astype(BF16), (((0,), (0,)), ((), ())),
                           preferred_element_type=F32)


def _fdot(a, b):
    return jnp.dot(a, b, preferred_element_type=F32, precision=lax.Precision.HIGHEST)


ColOff = collections.namedtuple("ColOff", ["x", "off"])


def _parts(x):
    return tuple(x) if isinstance(x, (tuple, list)) else (x,)


def _tile_starts(parts, tm):
    starts, s = [], 0
    for p in parts:
        assert p.shape[0] % tm == 0
        starts.append(s)
        s += p.shape[0] // tm
    return tuple(starts), s


def _part_spec(part, start, tm, cols, col_of, tile_arg):
    n = part.shape[0] // tm

    def index_map(*g):
        return (jnp.clip(g[tile_arg] - start, 0, n - 1), col_of(*g))

    return pl.BlockSpec((tm, cols), index_map)


def _select_part(refs, starts, i):
    v = refs[0][...]
    for r, s in zip(refs[1:], starts[1:]):
        v = jnp.where(i >= s, r[...], v)
    return v


def _rms_kernel(*refs, in_starts, out_starts):
    i = pl.program_id(0)
    x_refs = refs[:len(in_starts)]
    g_ref = refs[len(in_starts)]
    o_refs = refs[len(in_starts) + 1:]
    x = _select_part(x_refs, in_starts, i)
    y = x * lax.rsqrt(jnp.mean(x * x, axis=-1, keepdims=True) + EPS) * g_ref[...]
    if len(o_refs) == 1:
        o_refs[0][...] = y.astype(o_refs[0].dtype)
        return
    bounds = list(out_starts[1:]) + [None]
    for o_ref, lo, hi in zip(o_refs, out_starts, bounds):
        cond = (i >= lo) if hi is None else ((i >= lo) & (i < hi))

        @pl.when(cond)
        def _():
            o_ref[...] = y.astype(o_ref.dtype)


def rms_rows(x, gamma, out_dtype, tm=512, split_rows=None):
    parts = _parts(x)
    d = parts[0].shape[1]
    in_starts, n_tiles = _tile_starts(parts, tm)
    in_specs = [_part_spec(p, st, tm, d, lambda i: 0, 0) for p, st in zip(parts, in_starts)]
    in_specs.append(pl.BlockSpec((1, d), lambda i: (0, 0)))
    if split_rows is None:
        out_starts = (0,)
        out_specs = pl.BlockSpec((tm, d), lambda i: (i, 0))
        out_shape = jax.ShapeDtypeStruct((n_tiles * tm, d), out_dtype)
    else:
        shapes = [jax.ShapeDtypeStruct((r, d), out_dtype) for r in split_rows]
        out_starts, total = _tile_starts(shapes, tm)
        assert total == n_tiles
        out_specs = [_part_spec(sh, st, tm, d, lambda i: 0, 0) for sh, st in zip(shapes, out_starts)]
        out_shape = shapes
    return pl.pallas_call(
        functools.partial(_rms_kernel, in_starts=in_starts, out_starts=tuple(out_starts)),
        grid=(n_tiles,),
        in_specs=in_specs,
        out_specs=out_specs,
        out_shape=out_shape,
        compiler_params=_cp(("arbitrary",)),
        name="rms_rows",
    )(*parts, gamma.reshape(1, d))


def _fused_mm_kernel(*refs, a_starts, e_starts, epilogue):
    i = pl.program_id(1)
    pos = 0
    a_groups = []
    for st in a_starts:
        a_groups.append(refs[pos:pos + len(st)])
        pos += len(st)
    w_refs = refs[pos:pos + len(a_starts)]
    pos += len(a_starts)
    e_groups = []
    for st in e_starts:
        e_groups.append(refs[pos:pos + len(st)])
        pos += len(st)
    o_ref = refs[pos]
    wb_refs = refs[pos + 1:]

    @pl.when(i == 0)
    def _():
        for w_ref, wb_ref in zip(w_refs, wb_refs):
            wb_ref[...] = w_ref[...].astype(BF16)

    dots = [jnp.dot(_select_part(g, st, i).astype(BF16), wb_ref[...], preferred_element_type=F32)
            for g, st, wb_ref in zip(a_groups, a_starts, wb_refs)]
    extras = [_select_part(g, st, i) for g, st in zip(e_groups, e_starts)]
    o_ref[...] = epilogue(dots, extras).astype(o_ref.dtype)


def fused_mm(pairs, extras, epilogue, n_out, out_dtype, tm, tn, name):
    in_specs, args, scratch, a_starts, e_starts = [], [], [], [], []
    n_tiles = None
    for a, _, _ in pairs:
        parts = _parts(a)
        starts, total = _tile_starts(parts, tm)
        n_tiles = total if n_tiles is None else n_tiles
        assert total == n_tiles
        a_starts.append(starts)
        for p, st in zip(parts, starts):
            in_specs.append(_part_spec(p, st, tm, p.shape[1], lambda j, i: 0, 1))
            args.append(p)
    for a, w, off in pairs:
        k = _parts(a)[0].shape[1]
        assert off % tn == 0 and w.shape[0] == k and n_out % tn == 0
        in_specs.append(pl.BlockSpec((k, tn), functools.partial(lambda j, i, o: (0, j + o), o=off // tn)))
        args.append(w)
        scratch.append(pltpu.VMEM((k, tn), BF16))
    for e in extras:
        e, off = (e.x, e.off) if isinstance(e, ColOff) else (e, 0)
        assert off % tn == 0
        parts = _parts(e)
        starts, total = _tile_starts(parts, tm)
        assert total == n_tiles
        e_starts.append(starts)
        for p, st in zip(parts, starts):
            in_specs.append(_part_spec(p, st, tm, tn, functools.partial(lambda j, i, o: j + o, o=off // tn), 1))
            args.append(p)
    kern = functools.partial(_fused_mm_kernel, a_starts=tuple(a_starts), e_starts=tuple(e_starts),
                             epilogue=epilogue)
    return pl.pallas_call(
        kern,
        grid=(n_out // tn, n_tiles),
        in_specs=in_specs,
        out_specs=pl.BlockSpec((tm, tn), lambda j, i: (i, j)),
        out_shape=jax.ShapeDtypeStruct((n_tiles * tm, n_out), out_dtype),
        scratch_shapes=scratch,
        compiler_params=_cp(("arbitrary", "arbitrary")),
        name=name,
    )(*args)


def _swa_prompt_kernel(sink_ref, q_ref, kp_ref, kc_ref, vp_ref, vc_ref, o_ref):
    j = pl.program_id(1)
    w = WINDOW
    qi = lax.broadcasted_iota(I32, (w, 2 * w), 0)
    km = lax.broadcasted_iota(I32, (w, 2 * w), 1)
    band = (km >= qi) & (km <= w + qi) & ((j > 0) | (km >= w))
    mask4 = jnp.concatenate([band] * GROUP, axis=0)
    lane = lax.broadcasted_iota(I32, (1, LANES), 1)
    lo = lane < HEAD_DIM
    scale = HEAD_DIM ** -0.5
    heads = range(N_KV_HEADS)

    def both_halves(prev_ref, cur_ref, h):
        pair = (h // 2) * LANES
        x = jnp.concatenate([prev_ref[:, pair:pair + LANES], cur_ref[:, pair:pair + LANES]], axis=0)
        x = jnp.where(lo if h % 2 == 0 else ~lo, x, 0.0)
        return x + pltpu.roll(x, HEAD_DIM, axis=1)

    def q_group(h):
        qs = []
        for g in range(GROUP):
            n = h * GROUP + g
            qp = q_ref[:, (n // 2) * LANES:(n // 2 + 1) * LANES]
            qs.append(jnp.where(lo if n % 2 == 0 else ~lo, qp, 0.0))
        return jnp.concatenate(qs, axis=0)

    k2 = [both_halves(kp_ref, kc_ref, h) for h in heads]
    v2 = [both_halves(vp_ref, vc_ref, h) for h in heads]
    q4 = [q_group(h) for h in heads]
    sink = [jnp.concatenate([jnp.full((w, 1), sink_ref[h * GROUP + g], F32) for g in range(GROUP)], axis=0)
            for h in heads]
    s = [jnp.where(mask4, _bdot_nt(q, k) * scale, NEG_BIG) for q, k in zip(q4, k2)]
    m = [jnp.maximum(jnp.max(x, axis=-1, keepdims=True), sk) for x, sk in zip(s, sink)]
    p = [jnp.exp(x - mm) for x, mm in zip(s, m)]
    den = [jnp.sum(x, axis=-1, keepdims=True) + jnp.exp(sk - mm) for x, sk, mm in zip(p, sink, m)]
    o4 = [_bdot(x, v) / dd for x, v, dd in zip(p, v2, den)]
    for h in heads:
        for g in range(0, GROUP, 2):
            n = h * GROUP + g
            pair_out = jnp.where(lo, o4[h][g * w:(g + 1) * w], o4[h][(g + 1) * w:(g + 2) * w])
            o_ref[:, (n // 2) * LANES:(n // 2 + 1) * LANES] = pair_out.astype(o_ref.dtype)


def swa_prompt(proj, sinks, batch, seq):
    nb = seq // WINDOW
    row = lambda b, j: b * nb + j
    return pl.pallas_call(
        _swa_prompt_kernel,
        grid=(batch, nb),
        in_specs=[
            pl.BlockSpec(memory_space=pltpu.SMEM),
            pl.BlockSpec((WINDOW, ATT_Q), lambda b, j: (row(b, j), 0)),
            pl.BlockSpec((WINDOW, ATT_KV), lambda b, j: (row(b, jnp.maximum(j - 1, 0)), ATT_Q // ATT_KV)),
            pl.BlockSpec((WINDOW, ATT_KV), lambda b, j: (row(b, j), ATT_Q // ATT_KV)),
            pl.BlockSpec((WINDOW, ATT_KV), lambda b, j: (row(b, jnp.maximum(j - 1, 0)), ATT_Q // ATT_KV + 1)),
            pl.BlockSpec((WINDOW, ATT_KV), lambda b, j: (row(b, j), ATT_Q // ATT_KV + 1)),
        ],
        out_specs=pl.BlockSpec((WINDOW, ATT_Q), lambda b, j: (row(b, j), 0)),
        out_shape=jax.ShapeDtypeStruct((batch * seq, ATT_Q), BF16),
        compiler_params=_cp(("parallel", "arbitrary")),
        name="swa_prompt",
    )(sinks, proj, proj, proj, proj, proj)


def _swa_sample_kernel(sink_ref, q_ref, kn_ref, vn_ref, ck_ref, cv_ref, o_ref, *, t_new, w_cache):
    bg = q_ref.shape[0]
    nb = bg * N_KV_HEADS
    rows = GROUP * t_new
    q = q_ref[...].reshape(nb, rows, HEAD_DIM)
    kn = kn_ref[...].reshape(nb, t_new, HEAD_DIM)
    vn = vn_ref[...].reshape(nb, t_new, HEAD_DIM)
    ck = ck_ref[...].reshape(nb, w_cache, HEAD_DIM)
    cv = cv_ref[...].reshape(nb, w_cache, HEAD_DIM)
    scale = HEAD_DIM ** -0.5
    bdims = (((2,), (2,)), ((0,), (0,)))
    s_c = lax.dot_general(q.astype(BF16), ck.astype(BF16), bdims, preferred_element_type=F32) * scale
    s_n = lax.dot_general(q.astype(BF16), kn.astype(BF16), bdims, preferred_element_type=F32) * scale
    qi_c = lax.broadcasted_iota(I32, (nb, rows, w_cache), 1) % t_new
    kc = lax.broadcasted_iota(I32, (nb, rows, w_cache), 2)
    s_c = jnp.where(kc >= w_cache + qi_c - WINDOW, s_c, NEG_BIG)
    qi_n = lax.broadcasted_iota(I32, (nb, rows, t_new), 1) % t_new
    kk = lax.broadcasted_iota(I32, (nb, rows, t_new), 2)
    s_n = jnp.where(kk <= qi_n, s_n, NEG_BIG)
    bi = lax.broadcasted_iota(I32, (nb, rows, 1), 0) % N_KV_HEADS
    gi = lax.broadcasted_iota(I32, (nb, rows, 1), 1) // t_new
    head = bi * GROUP + gi
    sink = jnp.zeros((nb, rows, 1), F32)
    for n in range(N_Q_HEADS):
        sink = jnp.where(head == n, sink_ref[n], sink)
    m = jnp.maximum(jnp.maximum(jnp.max(s_c, axis=-1, keepdims=True), jnp.max(s_n, axis=-1, keepdims=True)), sink)
    p_c = jnp.exp(s_c - m)
    p_n = jnp.exp(s_n - m)
    den = jnp.sum(p_c, axis=-1, keepdims=True) + jnp.sum(p_n, axis=-1, keepdims=True) + jnp.exp(sink - m)
    pv = (((2,), (1,)), ((0,), (0,)))
    o = (lax.dot_general(p_c.astype(BF16), cv.astype(BF16), pv, preferred_element_type=F32)
         + lax.dot_general(p_n.astype(BF16), vn.astype(BF16), pv, preferred_element_type=F32)) / den
    o_ref[...] = o.reshape(bg, N_KV_HEADS, rows, HEAD_DIM).astype(o_ref.dtype)


def swa_sample(q, kn, vn, ck, cv, sinks, bg=8):
    b, _, rows, _ = q.shape
    t_new, w_cache = kn.shape[2], ck.shape[2]
    spec = lambda r: pl.BlockSpec((bg, N_KV_HEADS, r, HEAD_DIM), lambda i: (i, 0, 0, 0))
    return pl.pallas_call(
        functools.partial(_swa_sample_kernel, t_new=t_new, w_cache=w_cache),
        grid=(b // bg,),
        in_specs=[pl.BlockSpec(memory_space=pltpu.SMEM), spec(rows), spec(t_new), spec(t_new),
                  spec(w_cache), spec(w_cache)],
        out_specs=spec(rows),
        out_shape=jax.ShapeDtypeStruct(q.shape, F32),
        compiler_params=_cp(("parallel",)),
        name="swa_sample",
    )(sinks, q, kn, vn, ck, cv)


def _unit_lower_inverses(a_list):
    c = a_list[0].shape[0]
    ri = lax.broadcasted_iota(I32, (c, c), 0)
    ci = lax.broadcasted_iota(I32, (c, c), 1)
    eye = jnp.where(ri == ci, 1.0, 0.0)
    ps = [-a for a in a_list]
    ts = [eye + p for p in ps]
    span = 2
    while span < c:
        ps = [_bdot(p, p) for p in ps]
        ts = [t + _bdot(t, p) for t, p in zip(ts, ps)]
        span *= 2
    return ts


def _conv_silu(ext_ref, x_ref, cw_ref, tb):
    ext_ref[SUBLANES:tb + SUBLANES, :] = x_ref[...]
    y = None
    for jj in range(CONV_W):
        term = ext_ref[SUBLANES - (CONV_W - 1) + jj:SUBLANES - (CONV_W - 1) + jj + tb, :] * cw_ref[jj:jj + 1, :]
        y = term if y is None else y + term
    ext_ref[0:SUBLANES, :] = ext_ref[tb:tb + SUBLANES, :]
    return y * _sigmoid(y)


def _l2norm(x):
    return x * lax.rsqrt(jnp.sum(x * x, axis=-1, keepdims=True) + EPS)


def _pad_a_lanes(v):
    return jnp.zeros((1, LANES), F32).at[0, N_DN_HEADS:2 * N_DN_HEADS].set(v)


def _gdn_prompt_kernel(q_ref, k_ref, v_ref, z_ref, ba_ref, cwq_ref, cwk_ref, cwv_ref, alog_ref, dtb_ref,
                       dnw_ref, o_ref, sout_ref, extq, extk, extv, s_scr, *, tb):
    hg = pl.program_id(1)
    t = pl.program_id(2)
    c = GDN_CHUNK
    nch = tb // c

    @pl.when(t == 0)
    def _():
        zero = jnp.zeros((SUBLANES, extq.shape[1]), F32)
        extq[0:SUBLANES, :] = zero
        extk[0:SUBLANES, :] = zero
        extv[0:SUBLANES, :] = zero
        s_scr[...] = jnp.zeros(s_scr.shape, F32)

    qc = _conv_silu(extq, q_ref, cwq_ref, tb)
    kc = _conv_silu(extk, k_ref, cwk_ref, tb)
    vc = _conv_silu(extv, v_ref, cwv_ref, tb)
    ba = ba_ref[...]
    lane = lax.broadcasted_iota(I32, (tb, LANES), 1)
    beta_all = _sigmoid(ba)
    g_all = -jnp.exp(alog_ref[...]) * _softplus(ba + dtb_ref[...])
    rt = lax.broadcasted_iota(I32, (tb, tb), 0)
    ct = lax.broadcasted_iota(I32, (tb, tb), 1)
    shift = c.bit_length() - 1
    same = jnp.right_shift(rt, shift) == jnp.right_shift(ct, shift)
    gcum_all = _fdot(jnp.where(same & (rt >= ct), 1.0, 0.0), g_all)
    ri = lax.broadcasted_iota(I32, (c, c), 0)
    ci = lax.broadcasted_iota(I32, (c, c), 1)
    incl = ri >= ci
    strict = ri > ci

    units = [(hh, ch) for hh in range(GDN_HG) for ch in range(nch)]
    qn_h, kn_h, v_h, beta_h, gc_h, grow_h = [], [], [], [], [], []
    for hh in range(GDN_HG):
        h = hg * GDN_HG + hh
        sl = slice(hh * DK, (hh + 1) * DK)
        qn_h.append(_l2norm(qc[:, sl]) * (DK ** -0.5))
        kn_h.append(_l2norm(kc[:, sl]))
        v_h.append(vc[:, sl])
        beta_h.append(jnp.sum(jnp.where(lane == h, beta_all, 0.0), axis=-1, keepdims=True))
        gcol = jnp.sum(jnp.where(lane == N_DN_HEADS + h, gcum_all, 0.0), axis=-1, keepdims=True)
        gb = jnp.broadcast_to(gcol, (tb, LANES))
        gc_h.append(gb)
        grow_h.append(gb.T)
    rows = lambda ch: slice(ch * c, (ch + 1) * c)
    qn = [qn_h[hh][rows(ch)] for hh, ch in units]
    kn = [kn_h[hh][rows(ch)] for hh, ch in units]
    vv = [v_h[hh][rows(ch)] for hh, ch in units]
    beta = [beta_h[hh][rows(ch)] for hh, ch in units]
    gc = [gc_h[hh][rows(ch)] for hh, ch in units]
    grow = [grow_h[hh][0:c, ch * c:(ch + 1) * c] for hh, ch in units]
    eg = [jnp.exp(g) for g in gc]
    kb = [k * b for k, b in zip(kn, beta)]
    vb = [v * b for v, b in zip(vv, beta)]
    kexp = [k * e for k, e in zip(kb, eg)]
    qexp = [q * e for q, e in zip(qn, eg)]
    glast = [g[c - 1:c, :] for g in gc]
    kend = [k * jnp.exp(gl - g) for k, gl, g in zip(kn, glast, gc)]
    sdec = [jnp.exp(gl) for gl in glast]
    ak = [_bdot_nt(jnp.concatenate([a, q], axis=0), k) for a, q, k in zip(kb, qn, kn)]
    decay = [jnp.where(incl, jnp.exp(jnp.where(incl, g[:, :c] - gr, 0.0)), 0.0) for g, gr in zip(gc, grow)]
    a_mat = [jnp.where(strict, x[:c] * d, 0.0) for x, d in zip(ak, decay)]
    qk = [jnp.where(incl, x[c:] * d, 0.0) for x, d in zip(ak, decay)]
    t_inv = _unit_lower_inverses(a_mat)
    uw = [_bdot(ti, jnp.concatenate([v, k], axis=1)) for ti, v, k in zip(t_inv, vb, kexp)]

    s = [s_scr[hh] for hh in range(GDN_HG)]
    outs = [[] for _ in range(GDN_HG)]
    for ch in range(nch):
        idx = [hh * nch + ch for hh in range(GDN_HG)]
        ws = [_bdot(jnp.concatenate([uw[i][:, DV:], qexp[i]], axis=0), s[hh]) for hh, i in enumerate(idx)]
        u = [uw[i][:, :DV] - w[:c] for i, w in zip(idx, ws)]
        o = [w[c:] + _bdot(qk[i], uu) for i, w, uu in zip(idx, ws, u)]
        s = [s[hh] * sdec[i] + _bdot_tn(kend[i], uu) for (hh, i), uu in zip(enumerate(idx), u)]
        for hh in range(GDN_HG):
            outs[hh].append(o[hh])
    dnw = dnw_ref[...]
    for hh in range(GDN_HG):
        s_scr[hh] = s[hh]
        sl = slice(hh * DK, (hh + 1) * DK)
        o = jnp.concatenate(outs[hh], axis=0)
        o = o * lax.rsqrt(jnp.mean(o * o, axis=-1, keepdims=True) + EPS) * dnw
        zh = z_ref[:, sl]
        o_ref[:, sl] = (o * (zh * _sigmoid(zh))).astype(o_ref.dtype)

    @pl.when(t == pl.num_programs(2) - 1)
    def _():
        sout_ref[0] = s_scr[...]


def gdn_prompt(proj, ba, conv_w, a_log, dt_bias, dn_norm, batch, seq):
    tb = GDN_TBLOCK
    nt = seq // tb
    wblk = GDN_HG * DK
    q0, k0, v0, z0 = (COL_QKV_D // wblk, (COL_QKV_D + N_DN_HEADS * DK) // wblk,
                      (COL_QKV_D + 2 * N_DN_HEADS * DK) // wblk, COL_Z // wblk)
    nhg = N_DN_HEADS // GDN_HG
    row = lambda b, t: b * nt + t
    col_spec = lambda c0: pl.BlockSpec((tb, wblk), lambda b, hg, t: (row(b, t), c0 + hg))
    cw_spec = lambda c0: pl.BlockSpec((CONV_W, wblk), lambda b, hg, t: (0, c0 + hg))
    vec = pl.BlockSpec((1, LANES), lambda b, hg, t: (0, 0))
    return pl.pallas_call(
        functools.partial(_gdn_prompt_kernel, tb=tb),
        grid=(batch, nhg, nt),
        in_specs=[col_spec(q0), col_spec(k0), col_spec(v0), col_spec(z0),
                  pl.BlockSpec((tb, LANES), lambda b, hg, t: (row(b, t), 0)),
                  cw_spec(0), cw_spec(nhg), cw_spec(2 * nhg), vec, vec, vec],
        out_specs=[pl.BlockSpec((tb, wblk), lambda b, hg, t: (row(b, t), hg)),
                   pl.BlockSpec((1, GDN_HG, DK, DV), lambda b, hg, t: (b, hg, 0, 0))],
        out_shape=[jax.ShapeDtypeStruct((batch * seq, DN_V), BF16),
                   jax.ShapeDtypeStruct((batch, N_DN_HEADS, DK, DV), F32)],
        scratch_shapes=[pltpu.VMEM((tb + SUBLANES, wblk), F32)] * 3 + [pltpu.VMEM((GDN_HG, DK, DV), F32)],
        compiler_params=_cp(("parallel", "parallel", "arbitrary")),
        name="gdn_prompt",
    )(proj, proj, proj, proj, ba, conv_w, conv_w, conv_w, _pad_a_lanes(a_log), _pad_a_lanes(dt_bias),
      dn_norm.reshape(1, DV))


def _gdn_sample_kernel(xp_ref, z_ref, ba_ref, cw_ref, alog_ref, dtb_ref, dnw_ref, s0_ref, o_ref, s_ref, *, t_new):
    bg = xp_ref.shape[0]
    rows = xp_ref.shape[1]
    ri = lax.broadcasted_iota(I32, (rows, LANES), 0)
    vmask = ri < t_new
    r8 = lax.broadcasted_iota(I32, (rows, rows), 0)
    c8 = lax.broadcasted_iota(I32, (rows, rows), 1)
    incl = r8 >= c8
    strict = r8 > c8
    ltri = jnp.where(incl, 1.0, 0.0)
    ones8 = jnp.ones((rows, rows), F32)
    dnw = dnw_ref[...]

    def per_batch(b, carry):
        xp = xp_ref[b]
        y = None
        for jj in range(CONV_W):
            sh = xp if jj == 0 else pltpu.roll(xp, rows - jj, axis=0)
            term = sh * cw_ref[jj:jj + 1, :]
            y = term if y is None else y + term
        y = y * _sigmoid(y)
        ba = ba_ref[b]
        beta_all = jnp.where(vmask, _sigmoid(ba), 0.0)
        g_all = jnp.where(vmask, -jnp.exp(alog_ref[...]) * _softplus(ba + dtb_ref[...]), 0.0)
        gcum_all = _fdot(ltri, g_all)
        z = z_ref[b]
        for h0 in range(0, N_DN_HEADS, GDN_SAMPLE_HEADS):
            hs = list(range(h0, h0 + GDN_SAMPLE_HEADS))
            qn = [_l2norm(y[:, h * DK:(h + 1) * DK]) * (DK ** -0.5) for h in hs]
            kn = [jnp.where(vmask, _l2norm(y[:, (N_DN_HEADS + h) * DK:(N_DN_HEADS + h + 1) * DK]), 0.0)
                  for h in hs]
            vh = [y[:, (2 * N_DN_HEADS + h) * DK:(2 * N_DN_HEADS + h + 1) * DK] for h in hs]
            beta = [beta_all[:, h:h + 1] for h in hs]
            gc = [jnp.broadcast_to(gcum_all[:, N_DN_HEADS + h:N_DN_HEADS + h + 1], (rows, LANES)) for h in hs]
            grow = [_fdot(ones8, jnp.where(r8 == c8, g[:, :rows], 0.0)) for g in gc]
            decay = [jnp.where(incl, jnp.exp(jnp.where(incl, g[:, :rows] - gr, 0.0)), 0.0)
                     for g, gr in zip(gc, grow)]
            eg = [jnp.exp(g) for g in gc]
            kb = [k * bt for k, bt in zip(kn, beta)]
            vb = [v * bt for v, bt in zip(vh, beta)]
            kexp = [k * e for k, e in zip(kb, eg)]
            qexp = [q * e for q, e in zip(qn, eg)]
            glast = [g[rows - 1:rows, :] for g in gc]
            kend = [k * jnp.exp(gl - g) for k, gl, g in zip(kn, glast, gc)]
            sdec = [jnp.exp(gl) for gl in glast]
            ak = [_bdot_nt(jnp.concatenate([a, q], axis=0), k) for a, q, k in zip(kb, qn, kn)]
            a_mat = [jnp.where(strict, x[:rows] * d, 0.0) for x, d in zip(ak, decay)]
            qk = [jnp.where(incl, x[rows:] * d, 0.0) for x, d in zip(ak, decay)]
            t_inv = _unit_lower_inverses(a_mat)
            uw = [_bdot(ti, jnp.concatenate([v, k], axis=1)) for ti, v, k in zip(t_inv, vb, kexp)]
            s = [s0_ref[b, h] for h in hs]
            ws = [_bdot(jnp.concatenate([x[:, DV:], q], axis=0), a) for x, q, a in zip(uw, qexp, s)]
            u = [x[:, :DV] - w[:rows] for x, w in zip(uw, ws)]
            o_rows = [w[rows:] + _bdot(m, uu) for w, m, uu in zip(ws, qk, u)]
            s = [a * sd + _bdot_tn(k, uu) for a, sd, k, uu in zip(s, sdec, kend, u)]
            for h, a, orow in zip(hs, s, o_rows):
                s_ref[b, h] = a
                o = orow * lax.rsqrt(jnp.mean(orow * orow, axis=-1, keepdims=True) + EPS) * dnw
                zh = z[:, h * DV:(h + 1) * DV]
                o_ref[b, :, h * DV:(h + 1) * DV] = (o * (zh * _sigmoid(zh))).astype(o_ref.dtype)
        return carry

    lax.fori_loop(0, bg, per_batch, 0)


def gdn_sample(xp, z, ba, conv_w, a_log, dt_bias, dn_norm, s0, t_new, bg=4):
    b = xp.shape[0]
    rows = xp.shape[1]
    full = lambda shape: pl.BlockSpec(shape, lambda i: (0,) * len(shape))
    return pl.pallas_call(
        functools.partial(_gdn_sample_kernel, t_new=t_new),
        grid=(b // bg,),
        in_specs=[pl.BlockSpec((bg, rows, CONV_DIM), lambda i: (i, 0, 0)),
                  pl.BlockSpec((bg, rows, DN_V), lambda i: (i, 0, 0)),
                  pl.BlockSpec((bg, rows, LANES), lambda i: (i, 0, 0)),
                  full((CONV_W, CONV_DIM)), full((1, LANES)), full((1, LANES)), full((1, DV)),
                  pl.BlockSpec((bg, N_DN_HEADS, DK, DV), lambda i: (i, 0, 0, 0))],
        out_specs=[pl.BlockSpec((bg, rows, DN_V), lambda i: (i, 0, 0)),
                   pl.BlockSpec((bg, N_DN_HEADS, DK, DV), lambda i: (i, 0, 0, 0))],
        out_shape=[jax.ShapeDtypeStruct((b, rows, DN_V), BF16),
                   jax.ShapeDtypeStruct((b, N_DN_HEADS, DK, DV), F32)],
        compiler_params=_cp(("parallel",)),
        name="gdn_sample",
    )(xp, z, ba, conv_w, _pad_a_lanes(a_log), _pad_a_lanes(dt_bias), dn_norm.reshape(1, DV), s0)


def _router_kernel(x_ref, g_ref, wr_ref, br_ref, mi_ref, mf_ref, cnt_ref, run_ref):
    i = pl.program_id(0)
    tm = x_ref.shape[0]

    @pl.when(i == 0)
    def _():
        run_ref[...] = jnp.zeros(run_ref.shape, F32)

    x = x_ref[...]
    xn = x * lax.rsqrt(jnp.mean(x * x, axis=-1, keepdims=True) + EPS) * g_ref[...]
    lane = lax.broadcasted_iota(I32, (tm, LANES), 1)
    lane_f = lane.astype(F32)
    logits = _fdot(xn, wr_ref[...]) + br_ref[...]
    work = jnp.where(lane < N_EXPERTS, logits, -jnp.inf)
    vals, idxs, hots = [], [], []
    for _ in range(TOP_K):
        mx = jnp.max(work, axis=-1, keepdims=True)
        idx_f = jnp.min(jnp.where(work == mx, lane_f, float(LANES)), axis=-1, keepdims=True)
        idx = idx_f.astype(I32)
        hot = lane == idx
        work = jnp.where(hot, -jnp.inf, work)
        vals.append(mx)
        idxs.append(idx)
        hots.append(hot)
    exps = [jnp.exp(v - vals[0]) for v in vals]
    den = exps[0] + exps[1] + exps[2] + exps[3]
    member = jnp.where(hots[0] | hots[1] | hots[2] | hots[3], 1.0, 0.0)
    ri = lax.broadcasted_iota(I32, (tm, tm), 0)
    ci = lax.broadcasted_iota(I32, (tm, tm), 1)
    before = _bdot(jnp.where(ri > ci, 1.0, 0.0), member) + run_ref[0:1, :]
    run_ref[...] = run_ref[...] + jnp.sum(member, axis=0, keepdims=True)
    mi = jnp.zeros((tm, LANES), I32)
    mf = jnp.zeros((tm, LANES), F32)
    for k in range(TOP_K):
        rank = jnp.sum(jnp.where(hots[k], before, 0.0), axis=-1, keepdims=True).astype(I32)
        mi = jnp.where(lane == k, idxs[k], mi)
        mi = jnp.where(lane == TOP_K + k, rank, mi)
        mf = jnp.where(lane == k, exps[k] / den, mf)
    mi_ref[...] = mi
    mf_ref[...] = mf
    cnt_ref[...] = run_ref[...]


def router(x, gamma, w_router, b_router):
    m, d = x.shape
    tm = TOK_TILE
    wr = jnp.zeros((d, LANES), F32).at[:, :N_EXPERTS].set(w_router)
    br = jnp.zeros((1, LANES), F32).at[0, :N_EXPERTS].set(b_router)
    return pl.pallas_call(
        _router_kernel,
        grid=(m // tm,),
        in_specs=[pl.BlockSpec((tm, d), lambda i: (i, 0)), pl.BlockSpec((1, d), lambda i: (0, 0)),
                  pl.BlockSpec((d, LANES), lambda i: (0, 0)), pl.BlockSpec((1, LANES), lambda i: (0, 0))],
        out_specs=[pl.BlockSpec((tm, LANES), lambda i: (i, 0)), pl.BlockSpec((tm, LANES), lambda i: (i, 0)),
                   pl.BlockSpec((SUBLANES, LANES), lambda i: (0, 0))],
        out_shape=[jax.ShapeDtypeStruct((m, LANES), I32), jax.ShapeDtypeStruct((m, LANES), F32),
                   jax.ShapeDtypeStruct((SUBLANES, LANES), F32)],
        scratch_shapes=[pltpu.VMEM((SUBLANES, LANES), F32)],
        compiler_params=_cp(("arbitrary",)),
        name="router",
    )(x, gamma.reshape(1, d), wr, br)


def _zero_kernel(sub_ref, o_ref):
    del sub_ref
    o_ref[...] = jnp.zeros(o_ref.shape, o_ref.dtype)


def zero_partial_subblocks(sub_idx, n_rows, d):
    return pl.pallas_call(
        _zero_kernel,
        grid_spec=pltpu.PrefetchScalarGridSpec(
            num_scalar_prefetch=1, grid=(sub_idx.shape[0],), in_specs=[],
            out_specs=pl.BlockSpec((MOE_SUB, d), lambda e, sub: (sub[e], 0))),
        out_shape=jax.ShapeDtypeStruct((n_rows, d), F32),
        compiler_params=_cp(("arbitrary",)),
        name="moe_zero_tails",
    )(sub_idx)


def _row_copy(src_ref, src_row, dst_ref, dst_row, sem):
    return pltpu.make_async_copy(src_ref.at[pl.ds(src_row, 1), :], dst_ref.at[pl.ds(dst_row, 1), :], sem)


def _load_dest(dest_ref, dest_smem, sem):
    cp = pltpu.make_async_copy(dest_ref, dest_smem, sem)
    cp.start()
    cp.wait()


def _for_each_assignment(dest_smem, fn):
    per_row = LANES // TOP_K

    def body(g, c):
        for j in range(per_row):
            for k in range(TOP_K):
                fn(g * per_row + j, k, dest_smem[g, j * TOP_K + k])
        return c

    lax.fori_loop(0, dest_smem.shape[0], body, 0)


def _drain_rows(copy, dest_smem):
    def body(g, c):
        for _ in range(LANES):
            copy.wait()
        return c

    lax.fori_loop(0, dest_smem.shape[0], body, 0)


def _dispatch_kernel(dest_ref, x_ref, g_ref, xs_in_ref, xs_ref, xn_ref, dest_smem, sem_idx, sem):
    del xs_in_ref
    _load_dest(dest_ref, dest_smem, sem_idx)
    x = x_ref[...]
    xn_ref[...] = x * lax.rsqrt(jnp.mean(x * x, axis=-1, keepdims=True) + EPS) * g_ref[...]
    _for_each_assignment(
        dest_smem, lambda r, k, dst: _row_copy(xn_ref, r, xs_ref, dst, sem).start(priority=k % 2))
    _drain_rows(_row_copy(xn_ref, 0, xs_ref, 0, sem), dest_smem)


def dispatch(x, gamma, dest2d, xs):
    m, d = x.shape
    tm = TOK_TILE
    drows = tm * TOP_K // LANES
    return pl.pallas_call(
        _dispatch_kernel,
        grid=(m // tm,),
        in_specs=[pl.BlockSpec((drows, LANES), lambda i: (i, 0)),
                  pl.BlockSpec((tm, d), lambda i: (i, 0)),
                  pl.BlockSpec((1, d), lambda i: (0, 0)),
                  pl.BlockSpec(memory_space=pl.ANY)],
        out_specs=pl.BlockSpec(memory_space=pl.ANY),
        out_shape=jax.ShapeDtypeStruct(xs.shape, xs.dtype),
        scratch_shapes=[pltpu.VMEM((tm, d), F32), pltpu.SMEM((drows, LANES), I32), pltpu.SemaphoreType.DMA,
                        pltpu.SemaphoreType.DMA],
        input_output_aliases={3: 0},
        compiler_params=_cp(("arbitrary",)),
        name="moe_dispatch",
    )(dest2d, x, gamma.reshape(1, d), xs)


def _expert_kernel(blk_ref, bexp_ref, nsub_ref, last_ref, x0_ref, x1_ref, x2_ref, x3_ref, wg_ref, wu_ref,
                   bg_ref, bu_ref, wo_ref, bo_ref, o_ref, xn_ref, wgb_ref, wub_ref, wob_ref):
    del blk_ref, bexp_ref, last_ref
    b = pl.program_id(0)
    f = pl.program_id(1)
    n = nsub_ref[b]
    x_refs = (x0_ref, x1_ref, x2_ref, x3_ref)

    @pl.when(f == 0)
    def _():
        for s, x_ref in enumerate(x_refs):
            rows = slice(s * MOE_SUB, (s + 1) * MOE_SUB)

            @pl.when(s < n)
            def _():
                xn_ref[rows, :] = x_ref[...].astype(BF16)
                o_ref[rows, :] = jnp.broadcast_to(bo_ref[...], (MOE_SUB, o_ref.shape[1]))

            @pl.when((n > 0) & (s >= n))
            def _():
                o_ref[rows, :] = jnp.zeros((MOE_SUB, o_ref.shape[1]), F32)

    @pl.when(n > 0)
    def _():
        wgb_ref[...] = wg_ref[...].astype(BF16)
        wub_ref[...] = wu_ref[...].astype(BF16)
        wob_ref[...] = wo_ref[...].astype(BF16)

    def run(subs):
        rows = [pl.ds(pl.multiple_of(s * MOE_SUB, MOE_SUB), MOE_SUB) for s in subs]
        xs = [xn_ref[r, :] for r in rows]
        gate = [jnp.dot(x, wgb_ref[...], preferred_element_type=F32) + bg_ref[...] for x in xs]
        up = [jnp.dot(x, wub_ref[...], preferred_element_type=F32) + bu_ref[...] for x in xs]
        gate = [jnp.minimum(g, SWIGLU_LIMIT) for g in gate]
        up = [jnp.clip(u, -SWIGLU_LIMIT, SWIGLU_LIMIT) for u in up]
        act = [((u + 1.0) * g * _sigmoid(SWIGLU_ALPHA * g)).astype(BF16) for u, g in zip(up, gate)]
        for r, a in zip(rows, act):
            o_ref[r, :] += jnp.dot(a, wob_ref[...], preferred_element_type=F32)

    def pair(p, c):
        run([2 * p, 2 * p + 1])
        return c

    lax.fori_loop(0, lax.shift_right_logical(n, 1), pair, 0)

    @pl.when(lax.bitwise_and(n, 1) == 1)
    def _():
        run([n - 1])


def experts(xs, w_in, b_in, w_out, b_out, blk, bexp, nsub, last_sub):
    n_rows, d = xs.shape
    nb = blk.shape[0]
    nf = D_FF // MOE_TF
    per = MOE_ROWS // MOE_SUB
    assert per == 4

    def x_spec(s):
        return pl.BlockSpec((MOE_SUB, d),
                            lambda b, f, blk, bexp, nsub, last: (jnp.minimum(per * blk[b] + s, last[b]), 0))

    wsel = lambda b, f, blk, bexp, nsub, last: bexp[b]
    in_specs = [x_spec(0), x_spec(1), x_spec(2), x_spec(3),
                pl.BlockSpec((None, d, MOE_TF), lambda b, f, *p: (wsel(b, f, *p), 0, f)),
                pl.BlockSpec((None, d, MOE_TF), lambda b, f, *p: (wsel(b, f, *p), 0, nf + f)),
                pl.BlockSpec((None, 1, MOE_TF), lambda b, f, *p: (wsel(b, f, *p), 0, f)),
                pl.BlockSpec((None, 1, MOE_TF), lambda b, f, *p: (wsel(b, f, *p), 0, nf + f)),
                pl.BlockSpec((None, MOE_TF, d), lambda b, f, *p: (wsel(b, f, *p), f, 0)),
                pl.BlockSpec((None, 1, d), lambda b, f, *p: (wsel(b, f, *p), 0, 0))]
    return pl.pallas_call(
        _expert_kernel,
        grid_spec=pltpu.PrefetchScalarGridSpec(
            num_scalar_prefetch=4, grid=(nb, nf), in_specs=in_specs,
            out_specs=pl.BlockSpec((MOE_ROWS, d), lambda b, f, blk, bexp, nsub, last: (blk[b], 0)),
            scratch_shapes=[pltpu.VMEM((MOE_ROWS, d), BF16), pltpu.VMEM((d, MOE_TF), BF16),
                            pltpu.VMEM((d, MOE_TF), BF16), pltpu.VMEM((MOE_TF, d), BF16)]),
        out_shape=jax.ShapeDtypeStruct((n_rows, d), F32),
        compiler_params=_cp(("arbitrary", "arbitrary")),
        name="moe_experts",
    )(blk, bexp, nsub, last_sub, xs, xs, xs, xs, w_in, w_in,
      b_in.reshape(N_EXPERTS, 1, 2 * D_FF), b_in.reshape(N_EXPERTS, 1, 2 * D_FF), w_out,
      b_out.reshape(N_EXPERTS, 1, d))


def _combine_kernel(dest_ref, x_ref, mf_ref, g_ref, ys_ref, o_ref, h_ref, buf, dest_smem, sem_idx, sem):
    _load_dest(dest_ref, dest_smem, sem_idx)
    _for_each_assignment(
        dest_smem, lambda r, k, src: _row_copy(ys_ref, src, buf.at[k], r, sem).start(priority=k % 2))
    _drain_rows(_row_copy(ys_ref, 0, buf.at[0], 0, sem), dest_smem)
    gates = mf_ref[...]
    acc = x_ref[...]
    for k in range(TOP_K):
        acc = acc + gates[:, k:k + 1] * buf[k]
    o_ref[...] = acc
    h = acc * lax.rsqrt(jnp.mean(acc * acc, axis=-1, keepdims=True) + EPS) * g_ref[...]
    h_ref[...] = h.astype(h_ref.dtype)


def combine(x, mf, dest2d, ys, gamma_next):
    m, d = x.shape
    tm = TOK_TILE
    drows = tm * TOP_K // LANES
    return pl.pallas_call(
        _combine_kernel,
        grid=(m // tm,),
        in_specs=[pl.BlockSpec((drows, LANES), lambda i: (i, 0)),
                  pl.BlockSpec((tm, d), lambda i: (i, 0)),
                  pl.BlockSpec((tm, LANES), lambda i: (i, 0)),
                  pl.BlockSpec((1, d), lambda i: (0, 0)),
                  pl.BlockSpec(memory_space=pl.ANY)],
        out_specs=[pl.BlockSpec((tm, d), lambda i: (i, 0)), pl.BlockSpec((tm, d), lambda i: (i, 0))],
        out_shape=[jax.ShapeDtypeStruct((m, d), F32), jax.ShapeDtypeStruct((m, d), BF16)],
        scratch_shapes=[pltpu.VMEM((TOP_K, tm, d), F32), pltpu.SMEM((drows, LANES), I32),
                        pltpu.SemaphoreType.DMA, pltpu.SemaphoreType.DMA],
        compiler_params=_cp(("arbitrary",)),
        name="moe_combine",
    )(dest2d, x, mf, gamma_next.reshape(1, d), ys)


def routed_experts_residual(x, gamma, w_router, b_router, w_exp_in, b_exp_in, w_exp_out, b_exp_out, gamma_next):
    m, d = x.shape
    mi, mf, cnt = router(x, gamma, w_router, b_router)
    idx = mi[:, :TOP_K]
    rank = mi[:, TOP_K:2 * TOP_K]
    counts = cnt[0, :N_EXPERTS].astype(I32)
    padded = (counts + MOE_ROWS - 1) // MOE_ROWS * MOE_ROWS
    pad_end = jnp.cumsum(padded)
    pad_start = pad_end - padded
    onehot = idx[:, :, None] == jnp.arange(N_EXPERTS, dtype=I32)[None, None, :]
    dest = jnp.sum(jnp.where(onehot, pad_start[None, None, :], 0), axis=-1) + rank
    dest2d = dest.reshape(m * TOP_K // LANES, LANES)
    n_blocks = (m * TOP_K + N_EXPERTS * (MOE_ROWS - 1) + MOE_ROWS - 1) // MOE_ROWS
    used = pad_end[-1] // MOE_ROWS
    bids = jnp.arange(n_blocks, dtype=I32)
    blk = jnp.minimum(bids, jnp.maximum(used - 1, 0))
    bexp = jnp.minimum(jnp.sum((pad_end[None, :] <= (blk * MOE_ROWS)[:, None]).astype(I32), axis=1), N_EXPERTS - 1)
    fill_end = pad_start + counts
    valid_rows = jnp.clip(fill_end[bexp] - blk * MOE_ROWS, 0, MOE_ROWS)
    nsub = jnp.where(bids < used, (valid_rows + MOE_SUB - 1) // MOE_SUB, 0).astype(I32)
    per = MOE_ROWS // MOE_SUB
    last_sub = (per * blk + jnp.maximum(nsub, 1) - 1).astype(I32)
    n_rows = n_blocks * MOE_ROWS
    tail_sub = jnp.minimum(fill_end // MOE_SUB, n_rows // MOE_SUB - 1).astype(I32)
    xs = zero_partial_subblocks(tail_sub, n_rows, d)
    xs = dispatch(x, gamma, dest2d, xs)
    ys = experts(xs, w_exp_in, b_exp_in, w_exp_out, b_exp_out, blk, bexp, nsub, last_sub)
    return combine(x, mf, dest2d, ys, gamma_next)


def _ep_plain(dots, extras):
    return dots[0]


def _ep_sigmoid(dots, extras):
    return _sigmoid(dots[0])


def _ep_merge(dots, extras):
    return extras[0] * dots[0] + extras[1] * dots[1]


def _ep_residual(dots, extras):
    return extras[0] + dots[0]


def _ep_ple(dots, extras):
    return extras[0] + _sigmoid(dots[0]) * dots[1]


def kernel(x_prompt, x_sample, cache_k, cache_v, state_conv, state_ssm, p_prompt, p_sample, w_in, sinks, conv_w, a_log, dt_bias, dn_norm, w_attn_branch, w_dn_branch, w_out, w_router, b_router, w_exp_in, b_exp_in, w_exp_out, b_exp_out, w_ple_proj, w_ple_gate, norm_mix, norm_moe, norm_ple, norm_final):
    depth = w_in.shape[0]
    bp, seq, d = x_prompt.shape
    bs, t_new, _ = x_sample.shape
    w_cache = cache_k.shape[2]
    mp, ms = bp * seq, bs * t_new
    x = (x_prompt.reshape(mp, d), x_sample.reshape(ms, d))
    p_prompt, p_sample = p_prompt.reshape(depth, mp, -1), p_sample.reshape(depth, ms, -1)
    m = mp + ms
    tm_split = 512
    assert mp % tm_split == 0 and ms % tm_split == 0
    tm_big = 1536 if m % 1536 == 0 else 512
    outs = {k: [] for k in ("kp", "vp", "cp", "sp", "ks", "vs", "cs", "ss")}
    for i in range(depth):
        h0 = rms_rows(x, norm_mix[i], BF16, tm_split)
        proj = fused_mm([(h0, w_in[i], 0)], [], _ep_plain, COL_BA, F32, tm_big, 512, "proj_main")
        w_ba = jnp.zeros((d, LANES), F32).at[:, :2 * N_DN_HEADS].set(w_in[i][:, COL_BA:COL_GATES])
        ba = fused_mm([(h0, w_ba, 0)], [], _ep_plain, LANES, F32, tm_big, LANES, "proj_ba")
        gates = fused_mm([(h0, w_in[i][:, COL_GATES:], 0)], [], _ep_sigmoid, 2 * d, F32, tm_big, 512, "proj_gates")

        o_a_p = swa_prompt(proj, sinks[i], bp, seq)
        ps = proj[mp:]
        q_s = ps[:, :ATT_Q].reshape(bs, t_new, N_KV_HEADS, GROUP, HEAD_DIM).transpose(0, 2, 3, 1, 4)
        q_s = q_s.reshape(bs, N_KV_HEADS, GROUP * t_new, HEAD_DIM)
        k_s = ps[:, ATT_Q:ATT_Q + ATT_KV].reshape(bs, t_new, N_KV_HEADS, HEAD_DIM)
        v_s = ps[:, ATT_Q + ATT_KV:COL_QKV_D].reshape(bs, t_new, N_KV_HEADS, HEAD_DIM)
        o_a_s = swa_sample(q_s, k_s.transpose(0, 2, 1, 3), v_s.transpose(0, 2, 1, 3),
                           cache_k[i].transpose(0, 2, 1, 3), cache_v[i].transpose(0, 2, 1, 3), sinks[i])
        o_a_s = o_a_s.reshape(bs, N_KV_HEADS, GROUP, t_new, HEAD_DIM).transpose(0, 3, 1, 2, 4).reshape(ms, ATT_Q)
        o_a = (o_a_p, o_a_s.astype(BF16))
        tail = lambda n, c0, c1: jnp.stack(
            [lax.slice(proj, ((b + 1) * seq - n, c0), ((b + 1) * seq, c1)) for b in range(bp)])
        outs["kp"].append(tail(WINDOW, ATT_Q, ATT_Q + ATT_KV).reshape(bp, WINDOW, N_KV_HEADS, HEAD_DIM))
        outs["vp"].append(tail(WINDOW, ATT_Q + ATT_KV, COL_QKV_D).reshape(bp, WINDOW, N_KV_HEADS, HEAD_DIM))
        outs["ks"].append(jnp.concatenate([cache_k[i], k_s], axis=1)[:, -w_cache:])
        outs["vs"].append(jnp.concatenate([cache_v[i], v_s], axis=1)[:, -w_cache:])

        o_d_p, s_p = gdn_prompt(proj, ba, conv_w[i], a_log[i], dt_bias[i], dn_norm[i], bp, seq)
        qkv_s = ps[:, COL_QKV_D:COL_Z].reshape(bs, t_new, CONV_DIM)
        rows = SUBLANES
        pad_rows = rows - (CONV_W - 1) - t_new
        xp_s = jnp.concatenate([state_conv[i], qkv_s, jnp.zeros((bs, pad_rows, CONV_DIM), F32)], axis=1)
        tok_pad = lambda a: jnp.pad(a.reshape(bs, t_new, -1), ((0, 0), (0, rows - t_new), (0, 0)))
        o_d_s, s_s = gdn_sample(xp_s, tok_pad(ps[:, COL_Z:COL_BA]), tok_pad(ba[mp:]), conv_w[i], a_log[i],
                                dt_bias[i], dn_norm[i], state_ssm[i], t_new)
        o_d = (o_d_p, o_d_s[:, :t_new].reshape(ms, DN_V))
        outs["cp"].append(tail(CONV_W - 1, COL_QKV_D, COL_Z))
        outs["cs"].append(jnp.concatenate([state_conv[i], qkv_s], axis=1)[:, -(CONV_W - 1):])
        outs["sp"].append(s_p)
        outs["ss"].append(s_s)

        merged = fused_mm([(o_a, w_attn_branch[i], 0), (o_d, w_dn_branch[i], 0)], [gates, ColOff(gates, d)],
                          _ep_merge, d, BF16, tm_split, 512, "branch_merge")
        x = fused_mm([(merged, w_out[i], 0)], [x], _ep_residual, d, F32, tm_split, 512, "out_proj")

        x, hp = routed_experts_residual(x, norm_moe[i], w_router[i], b_router[i], w_exp_in[i], b_exp_in[i],
                                        w_exp_out[i], b_exp_out[i], norm_ple[i])

        x = fused_mm([(hp, w_ple_gate[i], 0), ((p_prompt[i], p_sample[i]), w_ple_proj[i], 0)], [x], _ep_ple, d, F32,
                     tm_split, 512, "ple")
    y_p, y_s = rms_rows(x, norm_final, F32, tm_split, split_rows=(mp, ms))
    st = lambda k: jnp.stack(outs[k])
    return (y_p.reshape(bp, seq, d), y_s.reshape(bs, t_new, d), st("kp"), st("vp"), st("cp"), st("sp"),
            st("ks"), st("vs"), st("cs"), st("ss"))
```

```python
import collections
import functools

import jax
import jax.numpy as jnp
from jax import lax
from jax.experimental import pallas as pl
from jax.experimental.pallas import tpu as pltpu

F32 = jnp.float32
BF16 = jnp.bfloat16
I32 = jnp.int32

D_MODEL = 2048
N_Q_HEADS = 16
N_KV_HEADS = 4
GROUP = N_Q_HEADS // N_KV_HEADS
HEAD_DIM = 64
WINDOW = 128
N_DN_HEADS = 16
DK = 128
DV = 128
CONV_W = 4
CONV_DIM = N_DN_HEADS * (2 * DK + DV)
N_EXPERTS = 32
TOP_K = 4
D_FF = D_MODEL
SWIGLU_LIMIT = 7.0
SWIGLU_ALPHA = 1.702
EPS = 1e-6
ATT_Q = N_Q_HEADS * HEAD_DIM
ATT_KV = N_KV_HEADS * HEAD_DIM
DN_V = N_DN_HEADS * DV

COL_QKV_D = ATT_Q + 2 * ATT_KV
COL_Z = COL_QKV_D + CONV_DIM
COL_BA = COL_Z + DN_V
COL_GATES = COL_BA + 2 * N_DN_HEADS

LANES = 128
SUBLANES = 8
VMEM_LIMIT_BYTES = 56 * 2 ** 20

GDN_CHUNK = 64
GDN_TBLOCK = 256
GDN_HG = 4
GDN_SAMPLE_HEADS = 8
MOE_ROWS = 1024
MOE_SUB = 256
MOE_TF = 256
MOE_KC = 512
TOK_TILE = 256
NEG_BIG = -1e30


def _cp(sem, vmem=VMEM_LIMIT_BYTES):
    return pltpu.CompilerParams(dimension_semantics=sem, vmem_limit_bytes=vmem)


def _sigmoid(x):
    return 1.0 / (1.0 + jnp.exp(-x))


def _softplus(x):
    return jnp.maximum(x, 0.0) + jnp.log1p(jnp.exp(-jnp.abs(x)))


def _bdot(a, b):
    return jnp.dot(a.astype(BF16), b.astype(BF16), preferred_element_type=F32)


def _bdot_nt(a, b):
    return lax.dot_general(a.astype(BF16), b.astype(BF16), (((1,), (1,)), ((), ())),
                           preferred_element_type=F32)


def _bdot_tn(a, b):
    return lax.dot_general(a.astype(BF16), b.astype(BF16), (((0,), (0,)), ((), ())),
                           preferred_element_type=F32)


def _fdot(a, b):
    return jnp.dot(a, b, preferred_element_type=F32, precision=lax.Precision.HIGHEST)


ColOff = collections.namedtuple("ColOff", ["x", "off"])


def _parts(x):
    return tuple(x) if isinstance(x, (tuple, list)) else (x,)


def _tile_starts(parts, tm):
    starts, s = [], 0
    for p in parts:
        assert p.shape[0] % tm == 0
        starts.append(s)
        s += p.shape[0] // tm
    return tuple(starts), s


def _part_spec(part, start, tm, cols, col_of, tile_arg):
    n = part.shape[0] // tm

    def index_map(*g):
        return (jnp.clip(g[tile_arg] - start, 0, n - 1), col_of(*g))

    return pl.BlockSpec((tm, cols), index_map)


def _select_part(refs, starts, i):
    v = refs[0][...]
    for r, s in zip(refs[1:], starts[1:]):
        v = jnp.where(i >= s, r[...], v)
    return v


def _rms_kernel(*refs, in_starts, out_starts):
    i = pl.program_id(0)
    x_refs = refs[:len(in_starts)]
    g_ref = refs[len(in_starts)]
    o_refs = refs[len(in_starts) + 1:]
    x = _select_part(x_refs, in_starts, i)
    y = x * lax.rsqrt(jnp.mean(x * x, axis=-1, keepdims=True) + EPS) * g_ref[...]
    if len(o_refs) == 1:
        o_refs[0][...] = y.astype(o_refs[0].dtype)
        return
    bounds = list(out_starts[1:]) + [None]
    for o_ref, lo, hi in zip(o_refs, out_starts, bounds):
        cond = (i >= lo) if hi is None else ((i >= lo) & (i < hi))

        @pl.when(cond)
        def _():
            o_ref[...] = y.astype(o_ref.dtype)


def rms_rows(x, gamma, out_dtype, tm=512, split_rows=None):
    parts = _parts(x)
    d = parts[0].shape[1]
    in_starts, n_tiles = _tile_starts(parts, tm)
    in_specs = [_part_spec(p, st, tm, d, lambda i: 0, 0) for p, st in zip(parts, in_starts)]
    in_specs.append(pl.BlockSpec((1, d), lambda i: (0, 0)))
    if split_rows is None:
        out_starts = (0,)
        out_specs = pl.BlockSpec((tm, d), lambda i: (i, 0))
        out_shape = jax.ShapeDtypeStruct((n_tiles * tm, d), out_dtype)
    else:
        shapes = [jax.ShapeDtypeStruct((r, d), out_dtype) for r in split_rows]
        out_starts, total = _tile_starts(shapes, tm)
        assert total == n_tiles
        out_specs = [_part_spec(sh, st, tm, d, lambda i: 0, 0) for sh, st in zip(shapes, out_starts)]
        out_shape = shapes
    return pl.pallas_call(
        functools.partial(_rms_kernel, in_starts=in_starts, out_starts=tuple(out_starts)),
        grid=(n_tiles,),
        in_specs=in_specs,
        out_specs=out_specs,
        out_shape=out_shape,
        compiler_params=_cp(("arbitrary",)),
        name="rms_rows",
    )(*parts, gamma.reshape(1, d))


def _fused_mm_kernel(*refs, a_starts, e_starts, epilogue):
    i = pl.program_id(1)
    pos = 0
    a_groups = []
    for st in a_starts:
        a_groups.append(refs[pos:pos + len(st)])
        pos += len(st)
    w_refs = refs[pos:pos + len(a_starts)]
    pos += len(a_starts)
    e_groups = []
    for st in e_starts:
        e_groups.append(refs[pos:pos + len(st)])
        pos += len(st)
    o_ref = refs[pos]
    wb_refs = refs[pos + 1:]

    @pl.when(i == 0)
    def _():
        for w_ref, wb_ref in zip(w_refs, wb_refs):
            wb_ref[...] = w_ref[...].astype(BF16)

    dots = [jnp.dot(_select_part(g, st, i).astype(BF16), wb_ref[...], preferred_element_type=F32)
            for g, st, wb_ref in zip(a_groups, a_starts, wb_refs)]
    extras = [_select_part(g, st, i) for g, st in zip(e_groups, e_starts)]
    o_ref[...] = epilogue(dots, extras).astype(o_ref.dtype)


def fused_mm(pairs, extras, epilogue, n_out, out_dtype, tm, tn, name):
    in_specs, args, scratch, a_starts, e_starts = [], [], [], [], []
    n_tiles = None
    for a, _, _ in pairs:
        parts = _parts(a)
        starts, total = _tile_starts(parts, tm)
        n_tiles = total if n_tiles is None else n_tiles
        assert total == n_tiles
        a_starts.append(starts)
        for p, st in zip(parts, starts):
            in_specs.append(_part_spec(p, st, tm, p.shape[1], lambda j, i: 0, 1))
            args.append(p)
    for a, w, off in pairs:
        k = _parts(a)[0].shape[1]
        assert off % tn == 0 and w.shape[0] == k and n_out % tn == 0
        in_specs.append(pl.BlockSpec((k, tn), functools.partial(lambda j, i, o: (0, j + o), o=off // tn)))
        args.append(w)
        scratch.append(pltpu.VMEM((k, tn), BF16))
    for e in extras:
        e, off = (e.x, e.off) if isinstance(e, ColOff) else (e, 0)
        assert off % tn == 0
        parts = _parts(e)
        starts, total = _tile_starts(parts, tm)
        assert total == n_tiles
        e_starts.append(starts)
        for p, st in zip(parts, starts):
            in_specs.append(_part_spec(p, st, tm, tn, functools.partial(lambda j, i, o: j + o, o=off // tn), 1))
            args.append(p)
    kern = functools.partial(_fused_mm_kernel, a_starts=tuple(a_starts), e_starts=tuple(e_starts),
                             epilogue=epilogue)
    return pl.pallas_call(
        kern,
        grid=(n_out // tn, n_tiles),
        in_specs=in_specs,
        out_specs=pl.BlockSpec((tm, tn), lambda j, i: (i, j)),
        out_shape=jax.ShapeDtypeStruct((n_tiles * tm, n_out), out_dtype),
        scratch_shapes=scratch,
        compiler_params=_cp(("arbitrary", "arbitrary")),
        name=name,
    )(*args)


def _swa_prompt_kernel(sink_ref, q_ref, kp_ref, kc_ref, vp_ref, vc_ref, o_ref):
    j = pl.program_id(1)
    w = WINDOW
    qi = lax.broadcasted_iota(I32, (w, 2 * w), 0)
    km = lax.broadcasted_iota(I32, (w, 2 * w), 1)
    band = (km >= qi) & (km <= w + qi) & ((j > 0) | (km >= w))
    mask4 = jnp.concatenate([band] * GROUP, axis=0)
    lane = lax.broadcasted_iota(I32, (1, LANES), 1)
    lo = lane < HEAD_DIM
    scale = HEAD_DIM ** -0.5
    heads = range(N_KV_HEADS)

    def both_halves(prev_ref, cur_ref, h):
        pair = (h // 2) * LANES
        x = jnp.concatenate([prev_ref[:, pair:pair + LANES], cur_ref[:, pair:pair + LANES]], axis=0)
        x = jnp.where(lo if h % 2 == 0 else ~lo, x, 0.0)
        return x + pltpu.roll(x, HEAD_DIM, axis=1)

    def q_group(h):
        qs = []
        for g in range(GROUP):
            n = h * GROUP + g
            qp = q_ref[:, (n // 2) * LANES:(n // 2 + 1) * LANES]
            qs.append(jnp.where(lo if n % 2 == 0 else ~lo, qp, 0.0))
        return jnp.concatenate(qs, axis=0)

    k2 = [both_halves(kp_ref, kc_ref, h) for h in heads]
    v2 = [both_halves(vp_ref, vc_ref, h) for h in heads]
    q4 = [q_group(h) for h in heads]
    sink = [jnp.concatenate([jnp.full((w, 1), sink_ref[h * GROUP + g], F32) for g in range(GROUP)], axis=0)
            for h in heads]
    s = [jnp.where(mask4, _bdot_nt(q, k) * scale, NEG_BIG) for q, k in zip(q4, k2)]
    m = [jnp.maximum(jnp.max(x, axis=-1, keepdims=True), sk) for x, sk in zip(s, sink)]
    p = [jnp.exp(x - mm) for x, mm in zip(s, m)]
    den = [jnp.sum(x, axis=-1, keepdims=True) + jnp.exp(sk - mm) for x, sk, mm in zip(p, sink, m)]
    o4 = [_bdot(x, v) / dd for x, v, dd in zip(p, v2, den)]
    for h in heads:
        for g in range(0, GROUP, 2):
            n = h * GROUP + g
            pair_out = jnp.where(lo, o4[h][g * w:(g + 1) * w], o4[h][(g + 1) * w:(g + 2) * w])
            o_ref[:, (n // 2) * LANES:(n // 2 + 1) * LANES] = pair_out.astype(o_ref.dtype)


def swa_prompt(proj, sinks, batch, seq):
    nb = seq // WINDOW
    row = lambda b, j: b * nb + j
    return pl.pallas_call(
        _swa_prompt_kernel,
        grid=(batch, nb),
        in_specs=[
            pl.BlockSpec(memory_space=pltpu.SMEM),
            pl.BlockSpec((WINDOW, ATT_Q), lambda b, j: (row(b, j), 0)),
            pl.BlockSpec((WINDOW, ATT_KV), lambda b, j: (row(b, jnp.maximum(j - 1, 0)), ATT_Q // ATT_KV)),
            pl.BlockSpec((WINDOW, ATT_KV), lambda b, j: (row(b, j), ATT_Q // ATT_KV)),
            pl.BlockSpec((WINDOW, ATT_KV), lambda b, j: (row(b, jnp.maximum(j - 1, 0)), ATT_Q // ATT_KV + 1)),
            pl.BlockSpec((WINDOW, ATT_KV), lambda b, j: (row(b, j), ATT_Q // ATT_KV + 1)),
        ],
        out_specs=pl.BlockSpec((WINDOW, ATT_Q), lambda b, j: (row(b, j), 0)),
        out_shape=jax.ShapeDtypeStruct((batch * seq, ATT_Q), BF16),
        compiler_params=_cp(("parallel", "arbitrary")),
        name="swa_prompt",
    )(sinks, proj, proj, proj, proj, proj)


def _swa_sample_kernel(sink_ref, q_ref, kn_ref, vn_ref, ck_ref, cv_ref, o_ref, *, t_new, w_cache):
    bg = q_ref.shape[0]
    nb = bg * N_KV_HEADS
    rows = GROUP * t_new
    q = q_ref[...].reshape(nb, rows, HEAD_DIM)
    kn = kn_ref[...].reshape(nb, t_new, HEAD_DIM)
    vn = vn_ref[...].reshape(nb, t_new, HEAD_DIM)
    ck = ck_ref[...].reshape(nb, w_cache, HEAD_DIM)
    cv = cv_ref[...].reshape(nb, w_cache, HEAD_DIM)
    scale = HEAD_DIM ** -0.5
    bdims = (((2,), (2,)), ((0,), (0,)))
    s_c = lax.dot_general(q.astype(BF16), ck.astype(BF16), bdims, preferred_element_type=F32) * scale
    s_n = lax.dot_general(q.astype(BF16), kn.astype(BF16), bdims, preferred_element_type=F32) * scale
    qi_c = lax.broadcasted_iota(I32, (nb, rows, w_cache), 1) % t_new
    kc = lax.broadcasted_iota(I32, (nb, rows, w_cache), 2)
    s_c = jnp.where(kc >= w_cache + qi_c - WINDOW, s_c, NEG_BIG)
    qi_n = lax.broadcasted_iota(I32, (nb, rows, t_new), 1) % t_new
    kk = lax.broadcasted_iota(I32, (nb, rows, t_new), 2)
    s_n = jnp.where(kk <= qi_n, s_n, NEG_BIG)
    bi = lax.broadcasted_iota(I32, (nb, rows, 1), 0) % N_KV_HEADS
    gi = lax.broadcasted_iota(I32, (nb, rows, 1), 1) // t_new
    head = bi * GROUP + gi
    sink = jnp.zeros((nb, rows, 1), F32)
    for n in range(N_Q_HEADS):
        sink = jnp.where(head == n, sink_ref[n], sink)
    m = jnp.maximum(jnp.maximum(jnp.max(s_c, axis=-1, keepdims=True), jnp.max(s_n, axis=-1, keepdims=True)), sink)
    p_c = jnp.exp(s_c - m)
    p_n = jnp.exp(s_n - m)
    den = jnp.sum(p_c, axis=-1, keepdims=True) + jnp.sum(p_n, axis=-1, keepdims=True) + jnp.exp(sink - m)
    pv = (((2,), (1,)), ((0,), (0,)))
    o = (lax.dot_general(p_c.astype(BF16), cv.astype(BF16), pv, preferred_element_type=F32)
         + lax.dot_general(p_n.astype(BF16), vn.astype(BF16), pv, preferred_element_type=F32)) / den
    o_ref[...] = o.reshape(bg, N_KV_HEADS, rows, HEAD_DIM).astype(o_ref.dtype)


def swa_sample(q, kn, vn, ck, cv, sinks, bg=8):
    b, _, rows, _ = q.shape
    t_new, w_cache = kn.shape[2], ck.shape[2]
    spec = lambda r: pl.BlockSpec((bg, N_KV_HEADS, r, HEAD_DIM), lambda i: (i, 0, 0, 0))
    return pl.pallas_call(
        functools.partial(_swa_sample_kernel, t_new=t_new, w_cache=w_cache),
        grid=(b // bg,),
        in_specs=[pl.BlockSpec(memory_space=pltpu.SMEM), spec(rows), spec(t_new), spec(t_new),
                  spec(w_cache), spec(w_cache)],
        out_specs=spec(rows),
        out_shape=jax.ShapeDtypeStruct(q.shape, F32),
        compiler_params=_cp(("parallel",)),
        name="swa_sample",
    )(sinks, q, kn, vn, ck, cv)


def _unit_lower_inverses(a_list):
    c = a_list[0].shape[0]
    ri = lax.broadcasted_iota(I32, (c, c), 0)
    ci = lax.broadcasted_iota(I32, (c, c), 1)
    eye = jnp.where(ri == ci, 1.0, 0.0)
    ps = [-a for a in a_list]
    ts = [eye + p for p in ps]
    span = 2
    while span < c:
        ps = [_bdot(p, p) for p in ps]
        ts = [t + _bdot(t, p) for t, p in zip(ts, ps)]
        span *= 2
    return ts


def _conv_silu(ext_ref, x_ref, cw_ref, tb):
    ext_ref[SUBLANES:tb + SUBLANES, :] = x_ref[...]
    y = None
    for jj in range(CONV_W):
        term = ext_ref[SUBLANES - (CONV_W - 1) + jj:SUBLANES - (CONV_W - 1) + jj + tb, :] * cw_ref[jj:jj + 1, :]
        y = term if y is None else y + term
    ext_ref[0:SUBLANES, :] = ext_ref[tb:tb + SUBLANES, :]
    return y * _sigmoid(y)


def _l2norm(x):
    return x * lax.rsqrt(jnp.sum(x * x, axis=-1, keepdims=True) + EPS)


def _pad_a_lanes(v):
    return jnp.zeros((1, LANES), F32).at[0, N_DN_HEADS:2 * N_DN_HEADS].set(v)


def _gdn_prompt_kernel(q_ref, k_ref, v_ref, z_ref, ba_ref, cwq_ref, cwk_ref, cwv_ref, alog_ref, dtb_ref,
                       dnw_ref, o_ref, sout_ref, extq, extk, extv, s_scr, *, tb):
    hg = pl.program_id(1)
    t = pl.program_id(2)
    c = GDN_CHUNK
    nch = tb // c

    @pl.when(t == 0)
    def _():
        zero = jnp.zeros((SUBLANES, extq.shape[1]), F32)
        extq[0:SUBLANES, :] = zero
        extk[0:SUBLANES, :] = zero
        extv[0:SUBLANES, :] = zero
        s_scr[...] = jnp.zeros(s_scr.shape, F32)

    qc = _conv_silu(extq, q_ref, cwq_ref, tb)
    kc = _conv_silu(extk, k_ref, cwk_ref, tb)
    vc = _conv_silu(extv, v_ref, cwv_ref, tb)
    ba = ba_ref[...]
    lane = lax.broadcasted_iota(I32, (tb, LANES), 1)
    beta_all = _sigmoid(ba)
    g_all = -jnp.exp(alog_ref[...]) * _softplus(ba + dtb_ref[...])
    rt = lax.broadcasted_iota(I32, (tb, tb), 0)
    ct = lax.broadcasted_iota(I32, (tb, tb), 1)
    shift = c.bit_length() - 1
    same = jnp.right_shift(rt, shift) == jnp.right_shift(ct, shift)
    gcum_all = _fdot(jnp.where(same & (rt >= ct), 1.0, 0.0), g_all)
    ri = lax.broadcasted_iota(I32, (c, c), 0)
    ci = lax.broadcasted_iota(I32, (c, c), 1)
    incl = ri >= ci
    strict = ri > ci

    units = [(hh, ch) for hh in range(GDN_HG) for ch in range(nch)]
    qn_h, kn_h, v_h, beta_h, gc_h, grow_h = [], [], [], [], [], []
    for hh in range(GDN_HG):
        h = hg * GDN_HG + hh
        sl = slice(hh * DK, (hh + 1) * DK)
        qn_h.append(_l2norm(qc[:, sl]) * (DK ** -0.5))
        kn_h.append(_l2norm(kc[:, sl]))
        v_h.append(vc[:, sl])
        beta_h.append(jnp.sum(jnp.where(lane == h, beta_all, 0.0), axis=-1, keepdims=True))
        gcol = jnp.sum(jnp.where(lane == N_DN_HEADS + h, gcum_all, 0.0), axis=-1, keepdims=True)
        gb = jnp.broadcast_to(gcol, (tb, LANES))
        gc_h.append(gb)
        grow_h.append(gb.T)
    rows = lambda ch: slice(ch * c, (ch + 1) * c)
    qn = [qn_h[hh][rows(ch)] for hh, ch in units]
    kn = [kn_h[hh][rows(ch)] for hh, ch in units]
    vv = [v_h[hh][rows(ch)] for hh, ch in units]
    beta = [beta_h[hh][rows(ch)] for hh, ch in units]
    gc = [gc_h[hh][rows(ch)] for hh, ch in units]
    grow = [grow_h[hh][0:c, ch * c:(ch + 1) * c] for hh, ch in units]
    eg = [jnp.exp(g) for g in gc]
    kb = [k * b for k, b in zip(kn, beta)]
    vb = [v * b for v, b in zip(vv, beta)]
    kexp = [k * e for k, e in zip(kb, eg)]
    qexp = [q * e for q, e in zip(qn, eg)]
    glast = [g[c - 1:c, :] for g in gc]
    kend = [k * jnp.exp(gl - g) for k, gl, g in zip(kn, glast, gc)]
    sdec = [jnp.exp(gl) for gl in glast]
    ak = [_bdot_nt(jnp.concatenate([a, q], axis=0), k) for a, q, k in zip(kb, qn, kn)]
    decay = [jnp.where(incl, jnp.exp(jnp.where(incl, g[:, :c] - gr, 0.0)), 0.0) for g, gr in zip(gc, grow)]
    a_mat = [jnp.where(strict, x[:c] * d, 0.0) for x, d in zip(ak, decay)]
    qk = [jnp.where(incl, x[c:] * d, 0.0) for x, d in zip(ak, decay)]
    t_inv = _unit_lower_inverses(a_mat)
    uw = [_bdot(ti, jnp.concatenate([v, k], axis=1)) for ti, v, k in zip(t_inv, vb, kexp)]

    s = [s_scr[hh] for hh in range(GDN_HG)]
    outs = [[] for _ in range(GDN_HG)]
    for ch in range(nch):
        idx = [hh * nch + ch for hh in range(GDN_HG)]
        ws = [_bdot(jnp.concatenate([uw[i][:, DV:], qexp[i]], axis=0), s[hh]) for hh, i in enumerate(idx)]
        u = [uw[i][:, :DV] - w[:c] for i, w in zip(idx, ws)]
        o = [w[c:] + _bdot(qk[i], uu) for i, w, uu in zip(idx, ws, u)]
        s = [s[hh] * sdec[i] + _bdot_tn(kend[i], uu) for (hh, i), uu in zip(enumerate(idx), u)]
        for hh in range(GDN_HG):
            outs[hh].append(o[hh])
    dnw = dnw_ref[...]
    for hh in range(GDN_HG):
        s_scr[hh] = s[hh]
        sl = slice(hh * DK, (hh + 1) * DK)
        o = jnp.concatenate(outs[hh], axis=0)
        o = o * lax.rsqrt(jnp.mean(o * o, axis=-1, keepdims=True) + EPS) * dnw
        zh = z_ref[:, sl]
        o_ref[:, sl] = (o * (zh * _sigmoid(zh))).astype(o_ref.dtype)

    @pl.when(t == pl.num_programs(2) - 1)
    def _():
        sout_ref[0] = s_scr[...]


def gdn_prompt(proj, ba, conv_w, a_log, dt_bias, dn_norm, batch, seq):
    tb = GDN_TBLOCK
    nt = seq // tb
    wblk = GDN_HG * DK
    q0, k0, v0, z0 = (COL_QKV_D // wblk, (COL_QKV_D + N_DN_HEADS * DK) // wblk,
                      (COL_QKV_D + 2 * N_DN_HEADS * DK) // wblk, COL_Z // wblk)
    nhg = N_DN_HEADS // GDN_HG
    row = lambda b, t: b * nt + t
    col_spec = lambda c0: pl.BlockSpec((tb, wblk), lambda b, hg, t: (row(b, t), c0 + hg))
    cw_spec = lambda c0: pl.BlockSpec((CONV_W, wblk), lambda b, hg, t: (0, c0 + hg))
    vec = pl.BlockSpec((1, LANES), lambda b, hg, t: (0, 0))
    return pl.pallas_call(
        functools.partial(_gdn_prompt_kernel, tb=tb),
        grid=(batch, nhg, nt),
        in_specs=[col_spec(q0), col_spec(k0), col_spec(v0), col_spec(z0),
                  pl.BlockSpec((tb, LANES), lambda b, hg, t: (row(b, t), 0)),
                  cw_spec(0), cw_spec(nhg), cw_spec(2 * nhg), vec, vec, vec],
        out_specs=[pl.BlockSpec((tb, wblk), lambda b, hg, t: (row(b, t), hg)),
                   pl.BlockSpec((1, GDN_HG, DK, DV), lambda b, hg, t: (b, hg, 0, 0))],
        out_shape=[jax.ShapeDtypeStruct((batch * seq, DN_V), BF16),
                   jax.ShapeDtypeStruct((batch, N_DN_HEADS, DK, DV), F32)],
        scratch_shapes=[pltpu.VMEM((tb + SUBLANES, wblk), F32)] * 3 + [pltpu.VMEM((GDN_HG, DK, DV), F32)],
        compiler_params=_cp(("parallel", "parallel", "arbitrary")),
        name="gdn_prompt",
    )(proj, proj, proj, proj, ba, conv_w, conv_w, conv_w, _pad_a_lanes(a_log), _pad_a_lanes(dt_bias),
      dn_norm.reshape(1, DV))


def _gdn_sample_kernel(xp_ref, z_ref, ba_ref, cw_ref, alog_ref, dtb_ref, dnw_ref, s0_ref, o_ref, s_ref, *, t_new):
    bg = xp_ref.shape[0]
    rows = xp_ref.shape[1]
    ri = lax.broadcasted_iota(I32, (rows, LANES), 0)
    vmask = ri < t_new
    r8 = lax.broadcasted_iota(I32, (rows, rows), 0)
    c8 = lax.broadcasted_iota(I32, (rows, rows), 1)
    incl = r8 >= c8
    strict = r8 > c8
    ltri = jnp.where(incl, 1.0, 0.0)
    ones8 = jnp.ones((rows, rows), F32)
    dnw = dnw_ref[...]

    def per_batch(b, carry):
        xp = xp_ref[b]
        y = None
        for jj in range(CONV_W):
            sh = xp if jj == 0 else pltpu.roll(xp, rows - jj, axis=0)
            term = sh * cw_ref[jj:jj + 1, :]
            y = term if y is None else y + term
        y = y * _sigmoid(y)
        ba = ba_ref[b]
        beta_all = jnp.where(vmask, _sigmoid(ba), 0.0)
        g_all = jnp.where(vmask, -jnp.exp(alog_ref[...]) * _softplus(ba + dtb_ref[...]), 0.0)
        gcum_all = _fdot(ltri, g_all)
        z = z_ref[b]
        for h0 in range(0, N_DN_HEADS, GDN_SAMPLE_HEADS):
            hs = list(range(h0, h0 + GDN_SAMPLE_HEADS))
            qn = [_l2norm(y[:, h * DK:(h + 1) * DK]) * (DK ** -0.5) for h in hs]
            kn = [jnp.where(vmask, _l2norm(y[:, (N_DN_HEADS + h) * DK:(N_DN_HEADS + h + 1) * DK]), 0.0)
                  for h in hs]
            vh = [y[:, (2 * N_DN_HEADS + h) * DK:(2 * N_DN_HEADS + h + 1) * DK] for h in hs]
            beta = [beta_all[:, h:h + 1] for h in hs]
            gc = [jnp.broadcast_to(gcum_all[:, N_DN_HEADS + h:N_DN_HEADS + h + 1], (rows, LANES)) for h in hs]
            grow = [_fdot(ones8, jnp.where(r8 == c8, g[:, :rows], 0.0)) for g in gc]
            decay = [jnp.where(incl, jnp.exp(jnp.where(incl, g[:, :rows] - gr, 0.0)), 0.0)
                     for g, gr in zip(gc, grow)]
            eg = [jnp.exp(g) for g in gc]
            kb = [k * bt for k, bt in zip(kn, beta)]
            vb = [v * bt for v, bt in zip(vh, beta)]
            kexp = [k * e for k, e in zip(kb, eg)]
            qexp = [q * e for q, e in zip(qn, eg)]
            glast = [g[rows - 1:rows, :] for g in gc]
            kend = [k * jnp.exp(gl - g) for k, gl, g in zip(kn, glast, gc)]
            sdec = [jnp.exp(gl) for gl in glast]
            ak = [_bdot_nt(jnp.concatenate([a, q], axis=0), k) for a, q, k in zip(kb, qn, kn)]
            a_mat = [jnp.where(strict, x[:rows] * d, 0.0) for x, d in zip(ak, decay)]
            qk = [jnp.where(incl, x[rows:] * d, 0.0) for x, d in zip(ak, decay)]
            t_inv = _unit_lower_inverses(a_mat)
            uw = [_bdot(ti, jnp.concatenate([v, k], axis=1)) for ti, v, k in zip(t_inv, vb, kexp)]
            s = [s0_ref[b, h] for h in hs]
            ws = [_bdot(jnp.concatenate([x[:, DV:], q], axis=0), a) for x, q, a in zip(uw, qexp, s)]
            u = [x[:, :DV] - w[:rows] for x, w in zip(uw, ws)]
            o_rows = [w[rows:] + _bdot(m, uu) for w, m, uu in zip(ws, qk, u)]
            s = [a * sd + _bdot_tn(k, uu) for a, sd, k, uu in zip(s, sdec, kend, u)]
            for h, a, orow in zip(hs, s, o_rows):
                s_ref[b, h] = a
                o = orow * lax.rsqrt(jnp.mean(orow * orow, axis=-1, keepdims=True) + EPS) * dnw
                zh = z[:, h * DV:(h + 1) * DV]
                o_ref[b, :, h * DV:(h + 1) * DV] = (o * (zh * _sigmoid(zh))).astype(o_ref.dtype)
        return carry

    lax.fori_loop(0, bg, per_batch, 0)


def gdn_sample(xp, z, ba, conv_w, a_log, dt_bias, dn_norm, s0, t_new, bg=4):
    b = xp.shape[0]
    rows = xp.shape[1]
    full = lambda shape: pl.BlockSpec(shape, lambda i: (0,) * len(shape))
    return pl.pallas_call(
        functools.partial(_gdn_sample_kernel, t_new=t_new),
        grid=(b // bg,),
        in_specs=[pl.BlockSpec((bg, rows, CONV_DIM), lambda i: (i, 0, 0)),
                  pl.BlockSpec((bg, rows, DN_V), lambda i: (i, 0, 0)),
                  pl.BlockSpec((bg, rows, LANES), lambda i: (i, 0, 0)),
                  full((CONV_W, CONV_DIM)), full((1, LANES)), full((1, LANES)), full((1, DV)),
                  pl.BlockSpec((bg, N_DN_HEADS, DK, DV), lambda i: (i, 0, 0, 0))],
        out_specs=[pl.BlockSpec((bg, rows, DN_V), lambda i: (i, 0, 0)),
                   pl.BlockSpec((bg, N_DN_HEADS, DK, DV), lambda i: (i, 0, 0, 0))],
        out_shape=[jax.ShapeDtypeStruct((b, rows, DN_V), BF16),
                   jax.ShapeDtypeStruct((b, N_DN_HEADS, DK, DV), F32)],
        compiler_params=_cp(("parallel",)),
        name="gdn_sample",
    )(xp, z, ba, conv_w, _pad_a_lanes(a_log), _pad_a_lanes(dt_bias), dn_norm.reshape(1, DV), s0)


def _router_kernel(x_ref, g_ref, wr_ref, br_ref, mi_ref, mf_ref, cnt_ref, run_ref):
    i = pl.program_id(0)
    tm = x_ref.shape[0]

    @pl.when(i == 0)
    def _():
        run_ref[...] = jnp.zeros(run_ref.shape, F32)

    x = x_ref[...]
    xn = x * lax.rsqrt(jnp.mean(x * x, axis=-1, keepdims=True) + EPS) * g_ref[...]
    lane = lax.broadcasted_iota(I32, (tm, LANES), 1)
    lane_f = lane.astype(F32)
    logits = _fdot(xn, wr_ref[...]) + br_ref[...]
    work = jnp.where(lane < N_EXPERTS, logits, -jnp.inf)
    vals, idxs, hots = [], [], []
    for _ in range(TOP_K):
        mx = jnp.max(work, axis=-1, keepdims=True)
        idx_f = jnp.min(jnp.where(work == mx, lane_f, float(LANES)), axis=-1, keepdims=True)
        idx = idx_f.astype(I32)
        hot = lane == idx
        work = jnp.where(hot, -jnp.inf, work)
        vals.append(mx)
        idxs.append(idx)
        hots.append(hot)
    exps = [jnp.exp(v - vals[0]) for v in vals]
    den = exps[0] + exps[1] + exps[2] + exps[3]
    member = jnp.where(hots[0] | hots[1] | hots[2] | hots[3], 1.0, 0.0)
    ri = lax.broadcasted_iota(I32, (tm, tm), 0)
    ci = lax.broadcasted_iota(I32, (tm, tm), 1)
    before = _bdot(jnp.where(ri > ci, 1.0, 0.0), member) + run_ref[0:1, :]
    run_ref[...] = run_ref[...] + jnp.sum(member, axis=0, keepdims=True)
    mi = jnp.zeros((tm, LANES), I32)
    mf = jnp.zeros((tm, LANES), F32)
    for k in range(TOP_K):
        rank = jnp.sum(jnp.where(hots[k], before, 0.0), axis=-1, keepdims=True).astype(I32)
        mi = jnp.where(lane == k, idxs[k], mi)
        mi = jnp.where(lane == TOP_K + k, rank, mi)
        mf = jnp.where(lane == k, exps[k] / den, mf)
    mi_ref[...] = mi
    mf_ref[...] = mf
    cnt_ref[...] = run_ref[...]


def router(x, gamma, w_router, b_router):
    m, d = x.shape
    tm = TOK_TILE
    wr = jnp.zeros((d, LANES), F32).at[:, :N_EXPERTS].set(w_router)
    br = jnp.zeros((1, LANES), F32).at[0, :N_EXPERTS].set(b_router)
    return pl.pallas_call(
        _router_kernel,
        grid=(m // tm,),
        in_specs=[pl.BlockSpec((tm, d), lambda i: (i, 0)), pl.BlockSpec((1, d), lambda i: (0, 0)),
                  pl.BlockSpec((d, LANES), lambda i: (0, 0)), pl.BlockSpec((1, LANES), lambda i: (0, 0))],
        out_specs=[pl.BlockSpec((tm, LANES), lambda i: (i, 0)), pl.BlockSpec((tm, LANES), lambda i: (i, 0)),
                   pl.BlockSpec((SUBLANES, LANES), lambda i: (0, 0))],
        out_shape=[jax.ShapeDtypeStruct((m, LANES), I32), jax.ShapeDtypeStruct((m, LANES), F32),
                   jax.ShapeDtypeStruct((SUBLANES, LANES), F32)],
        scratch_shapes=[pltpu.VMEM((SUBLANES, LANES), F32)],
        compiler_params=_cp(("arbitrary",)),
        name="router",
    )(x, gamma.reshape(1, d), wr, br)


def _zero_kernel(sub_ref, o_ref):
    del sub_ref
    o_ref[...] = jnp.zeros(o_ref.shape, o_ref.dtype)


def zero_partial_subblocks(sub_idx, n_rows, d):
    return pl.pallas_call(
        _zero_kernel,
        grid_spec=pltpu.PrefetchScalarGridSpec(
            num_scalar_prefetch=1, grid=(sub_idx.shape[0],), in_specs=[],
            out_specs=pl.BlockSpec((MOE_SUB, d), lambda e, sub: (sub[e], 0))),
        out_shape=jax.ShapeDtypeStruct((n_rows, d), F32),
        compiler_params=_cp(("arbitrary",)),
        name="moe_zero_tails",
    )(sub_idx)


def _row_copy(src_ref, src_row, dst_ref, dst_row, sem):
    return pltpu.make_async_copy(src_ref.at[pl.ds(src_row, 1), :], dst_ref.at[pl.ds(dst_row, 1), :], sem)


def _load_dest(dest_ref, dest_smem, sem):
    cp = pltpu.make_async_copy(dest_ref, dest_smem, sem)
    cp.start()
    cp.wait()


def _for_each_assignment(dest_smem, fn):
    per_row = LANES // TOP_K

    def body(g, c):
        for j in range(per_row):
            for k in range(TOP_K):
                fn(g * per_row + j, k, dest_smem[g, j * TOP_K + k])
        return c

    lax.fori_loop(0, dest_smem.shape[0], body, 0)


def _drain_rows(copy, dest_smem):
    def body(g, c):
        for _ in range(LANES):
            copy.wait()
        return c

    lax.fori_loop(0, dest_smem.shape[0], body, 0)


def _dispatch_kernel(dest_ref, x_ref, g_ref, xs_in_ref, xs_ref, xn_ref, dest_smem, sem_idx, sem):
    del xs_in_ref
    _load_dest(dest_ref, dest_smem, sem_idx)
    x = x_ref[...]
    xn_ref[...] = x * lax.rsqrt(jnp.mean(x * x, axis=-1, keepdims=True) + EPS) * g_ref[...]
    _for_each_assignment(
        dest_smem, lambda r, k, dst: _row_copy(xn_ref, r, xs_ref, dst, sem).start(priority=k % 2))
    _drain_rows(_row_copy(xn_ref, 0, xs_ref, 0, sem), dest_smem)


def dispatch(x, gamma, dest2d, xs):
    m, d = x.shape
    tm = TOK_TILE
    drows = tm * TOP_K // LANES
    return pl.pallas_call(
        _dispatch_kernel,
        grid=(m // tm,),
        in_specs=[pl.BlockSpec((drows, LANES), lambda i: (i, 0)),
                  pl.BlockSpec((tm, d), lambda i: (i, 0)),
                  pl.BlockSpec((1, d), lambda i: (0, 0)),
                  pl.BlockSpec(memory_space=pl.ANY)],
        out_specs=pl.BlockSpec(memory_space=pl.ANY),
        out_shape=jax.ShapeDtypeStruct(xs.shape, xs.dtype),
        scratch_shapes=[pltpu.VMEM((tm, d), F32), pltpu.SMEM((drows, LANES), I32), pltpu.SemaphoreType.DMA,
                        pltpu.SemaphoreType.DMA],
        input_output_aliases={3: 0},
        compiler_params=_cp(("arbitrary",)),
        name="moe_dispatch",
    )(dest2d, x, gamma.reshape(1, d), xs)


def _expert_kernel(blk_ref, bexp_ref, nsub_ref, last_ref, x0_ref, x1_ref, x2_ref, x3_ref, wg_ref, wu_ref,
                   bg_ref, bu_ref, wo_ref, bo_ref, o_ref, xn_ref, wgb_ref, wub_ref, wob_ref):
    del blk_ref, bexp_ref, last_ref
    b = pl.program_id(0)
    f = pl.program_id(1)
    n = nsub_ref[b]
    x_refs = (x0_ref, x1_ref, x2_ref, x3_ref)

    @pl.when(f == 0)
    def _():
        for s, x_ref in enumerate(x_refs):
            rows = slice(s * MOE_SUB, (s + 1) * MOE_SUB)

            @pl.when(s < n)
            def _():
                xn_ref[rows, :] = x_ref[...].astype(BF16)
                o_ref[rows, :] = jnp.broadcast_to(bo_ref[...], (MOE_SUB, o_ref.shape[1]))

            @pl.when((n > 0) & (s >= n))
            def _():
                o_ref[rows, :] = jnp.zeros((MOE_SUB, o_ref.shape[1]), F32)

    @pl.when((n > 0) & (n < MOE_ROWS // MOE_SUB))
    def _():
        wgb_ref[...] = wg_ref[...].astype(BF16)
        wub_ref[...] = wu_ref[...].astype(BF16)
        wob_ref[...] = wo_ref[...].astype(BF16)

    def run(subs):
        rows = [pl.ds(pl.multiple_of(s * MOE_SUB, MOE_SUB), MOE_SUB) for s in subs]
        xs = [xn_ref[r, :] for r in rows]
        gate = [jnp.dot(x, wgb_ref[...], preferred_element_type=F32) + bg_ref[...] for x in xs]
        up = [jnp.dot(x, wub_ref[...], preferred_element_type=F32) + bu_ref[...] for x in xs]
        gate = [jnp.minimum(g, SWIGLU_LIMIT) for g in gate]
        up = [jnp.clip(u, -SWIGLU_LIMIT, SWIGLU_LIMIT) for u in up]
        act = [((u + 1.0) * g * _sigmoid(SWIGLU_ALPHA * g)).astype(BF16) for u, g in zip(up, gate)]
        for r, a in zip(rows, act):
            o_ref[r, :] += jnp.dot(a, wob_ref[...], preferred_element_type=F32)

    full = MOE_ROWS // MOE_SUB

    @pl.when(n == full)
    def _():
        gate = bg_ref[...]
        up = bu_ref[...]
        for k0 in range(0, xn_ref.shape[1], MOE_KC):
            xk = xn_ref[:, k0:k0 + MOE_KC]
            gate = gate + jnp.dot(xk, wg_ref[k0:k0 + MOE_KC, :].astype(BF16), preferred_element_type=F32)
            up = up + jnp.dot(xk, wu_ref[k0:k0 + MOE_KC, :].astype(BF16), preferred_element_type=F32)
        gate = jnp.minimum(gate, SWIGLU_LIMIT)
        up = jnp.clip(up, -SWIGLU_LIMIT, SWIGLU_LIMIT)
        act = ((up + 1.0) * gate * _sigmoid(SWIGLU_ALPHA * gate)).astype(BF16)
        o_ref[...] += jnp.dot(act, wo_ref[...].astype(BF16), preferred_element_type=F32)

    @pl.when(n < full)
    def _():
        def pair(p, c):
            run([2 * p, 2 * p + 1])
            return c

        lax.fori_loop(0, lax.shift_right_logical(n, 1), pair, 0)

        @pl.when(lax.bitwise_and(n, 1) == 1)
        def _():
            run([n - 1])


def experts(xs, w_in, b_in, w_out, b_out, blk, bexp, nsub, last_sub):
    n_rows, d = xs.shape
    nb = blk.shape[0]
    nf = D_FF // MOE_TF
    per = MOE_ROWS // MOE_SUB
    assert per == 4

    def x_spec(s):
        return pl.BlockSpec((MOE_SUB, d),
                            lambda b, f, blk, bexp, nsub, last: (jnp.minimum(per * blk[b] + s, last[b]), 0))

    wsel = lambda b, f, blk, bexp, nsub, last: bexp[b]
    in_specs = [x_spec(0), x_spec(1), x_spec(2), x_spec(3),
                pl.BlockSpec((None, d, MOE_TF), lambda b, f, *p: (wsel(b, f, *p), 0, f)),
                pl.BlockSpec((None, d, MOE_TF), lambda b, f, *p: (wsel(b, f, *p), 0, nf + f)),
                pl.BlockSpec((None, 1, MOE_TF), lambda b, f, *p: (wsel(b, f, *p), 0, f)),
                pl.BlockSpec((None, 1, MOE_TF), lambda b, f, *p: (wsel(b, f, *p), 0, nf + f)),
                pl.BlockSpec((None, MOE_TF, d), lambda b, f, *p: (wsel(b, f, *p), f, 0)),
                pl.BlockSpec((None, 1, d), lambda b, f, *p: (wsel(b, f, *p), 0, 0))]
    return pl.pallas_call(
        _expert_kernel,
        grid_spec=pltpu.PrefetchScalarGridSpec(
            num_scalar_prefetch=4, grid=(nb, nf), in_specs=in_specs,
            out_specs=pl.BlockSpec((MOE_ROWS, d), lambda b, f, blk, bexp, nsub, last: (blk[b], 0)),
            scratch_shapes=[pltpu.VMEM((MOE_ROWS, d), BF16), pltpu.VMEM((d, MOE_TF), BF16),
                            pltpu.VMEM((d, MOE_TF), BF16), pltpu.VMEM((MOE_TF, d), BF16)]),
        out_shape=jax.ShapeDtypeStruct((n_rows, d), F32),
        compiler_params=_cp(("arbitrary", "arbitrary")),
        name="moe_experts",
    )(blk, bexp, nsub, last_sub, xs, xs, xs, xs, w_in, w_in,
      b_in.reshape(N_EXPERTS, 1, 2 * D_FF), b_in.reshape(N_EXPERTS, 1, 2 * D_FF), w_out,
      b_out.reshape(N_EXPERTS, 1, d))


def _combine_kernel(dest_ref, x_ref, mf_ref, g_ref, ys_ref, o_ref, h_ref, buf, dest_smem, sem_idx, sem):
    _load_dest(dest_ref, dest_smem, sem_idx)
    _for_each_assignment(
        dest_smem, lambda r, k, src: _row_copy(ys_ref, src, buf.at[k], r, sem).start(priority=k % 2))
    _drain_rows(_row_copy(ys_ref, 0, buf.at[0], 0, sem), dest_smem)
    gates = mf_ref[...]
    acc = x_ref[...]
    for k in range(TOP_K):
        acc = acc + gates[:, k:k + 1] * buf[k]
    o_ref[...] = acc
    h = acc * lax.rsqrt(jnp.mean(acc * acc, axis=-1, keepdims=True) + EPS) * g_ref[...]
    h_ref[...] = h.astype(h_ref.dtype)


def combine(x, mf, dest2d, ys, gamma_next):
    m, d = x.shape
    tm = TOK_TILE
    drows = tm * TOP_K // LANES
    return pl.pallas_call(
        _combine_kernel,
        grid=(m // tm,),
        in_specs=[pl.BlockSpec((drows, LANES), lambda i: (i, 0)),
                  pl.BlockSpec((tm, d), lambda i: (i, 0)),
                  pl.BlockSpec((tm, LANES), lambda i: (i, 0)),
                  pl.BlockSpec((1, d), lambda i: (0, 0)),
                  pl.BlockSpec(memory_space=pl.ANY)],
        out_specs=[pl.BlockSpec((tm, d), lambda i: (i, 0)), pl.BlockSpec((tm, d), lambda i: (i, 0))],
        out_shape=[jax.ShapeDtypeStruct((m, d), F32), jax.ShapeDtypeStruct((m, d), BF16)],
        scratch_shapes=[pltpu.VMEM((TOP_K, tm, d), F32), pltpu.SMEM((drows, LANES), I32),
                        pltpu.SemaphoreType.DMA, pltpu.SemaphoreType.DMA],
        compiler_params=_cp(("arbitrary",)),
        name="moe_combine",
    )(dest2d, x, mf, gamma_next.reshape(1, d), ys)


def routed_experts_residual(x, gamma, w_router, b_router, w_exp_in, b_exp_in, w_exp_out, b_exp_out, gamma_next):
    m, d = x.shape
    mi, mf, cnt = router(x, gamma, w_router, b_router)
    idx = mi[:, :TOP_K]
    rank = mi[:, TOP_K:2 * TOP_K]
    counts = cnt[0, :N_EXPERTS].astype(I32)
    padded = (counts + MOE_ROWS - 1) // MOE_ROWS * MOE_ROWS
    pad_end = jnp.cumsum(padded)
    pad_start = pad_end - padded
    onehot = idx[:, :, None] == jnp.arange(N_EXPERTS, dtype=I32)[None, None, :]
    dest = jnp.sum(jnp.where(onehot, pad_start[None, None, :], 0), axis=-1) + rank
    dest2d = dest.reshape(m * TOP_K // LANES, LANES)
    n_blocks = (m * TOP_K + N_EXPERTS * (MOE_ROWS - 1) + MOE_ROWS - 1) // MOE_ROWS
    used = pad_end[-1] // MOE_ROWS
    bids = jnp.arange(n_blocks, dtype=I32)
    blk = jnp.minimum(bids, jnp.maximum(used - 1, 0))
    bexp = jnp.minimum(jnp.sum((pad_end[None, :] <= (blk * MOE_ROWS)[:, None]).astype(I32), axis=1), N_EXPERTS - 1)
    fill_end = pad_start + counts
    valid_rows = jnp.clip(fill_end[bexp] - blk * MOE_ROWS, 0, MOE_ROWS)
    nsub = jnp.where(bids < used, (valid_rows + MOE_SUB - 1) // MOE_SUB, 0).astype(I32)
    per = MOE_ROWS // MOE_SUB
    last_sub = (per * blk + jnp.maximum(nsub, 1) - 1).astype(I32)
    n_rows = n_blocks * MOE_ROWS
    tail_sub = jnp.minimum(fill_end // MOE_SUB, n_rows // MOE_SUB - 1).astype(I32)
    xs = zero_partial_subblocks(tail_sub, n_rows, d)
    xs = dispatch(x, gamma, dest2d, xs)
    ys = experts(xs, w_exp_in, b_exp_in, w_exp_out, b_exp_out, blk, bexp, nsub, last_sub)
    return combine(x, mf, dest2d, ys, gamma_next)


def _ep_plain(dots, extras):
    return dots[0]


def _ep_sigmoid(dots, extras):
    return _sigmoid(dots[0])


def _ep_merge(dots, extras):
    return extras[0] * dots[0] + extras[1] * dots[1]


def _ep_residual(dots, extras):
    return extras[0] + dots[0]


def _ep_ple(dots, extras):
    return extras[0] + _sigmoid(dots[0]) * dots[1]


def kernel(x_prompt, x_sample, cache_k, cache_v, state_conv, state_ssm, p_prompt, p_sample, w_in, sinks, conv_w, a_log, dt_bias, dn_norm, w_attn_branch, w_dn_branch, w_out, w_router, b_router, w_exp_in, b_exp_in, w_exp_out, b_exp_out, w_ple_proj, w_ple_gate, norm_mix, norm_moe, norm_ple, norm_final):
    depth = w_in.shape[0]
    bp, seq, d = x_prompt.shape
    bs, t_new, _ = x_sample.shape
    w_cache = cache_k.shape[2]
    mp, ms = bp * seq, bs * t_new
    x = (x_prompt.reshape(mp, d), x_sample.reshape(ms, d))
    p_prompt, p_sample = p_prompt.reshape(depth, mp, -1), p_sample.reshape(depth, ms, -1)
    m = mp + ms
    tm_split = 512
    assert mp % tm_split == 0 and ms % tm_split == 0
    tm_big = 1536 if m % 1536 == 0 else 512
    outs = {k: [] for k in ("kp", "vp", "cp", "sp", "ks", "vs", "cs", "ss")}
    for i in range(depth):
        h0 = rms_rows(x, norm_mix[i], BF16, tm_split)
        proj = fused_mm([(h0, w_in[i], 0)], [], _ep_plain, COL_BA, F32, tm_big, 512, "proj_main")
        w_ba = jnp.zeros((d, LANES), F32).at[:, :2 * N_DN_HEADS].set(w_in[i][:, COL_BA:COL_GATES])
        ba = fused_mm([(h0, w_ba, 0)], [], _ep_plain, LANES, F32, tm_big, LANES, "proj_ba")
        gates = fused_mm([(h0, w_in[i][:, COL_GATES:], 0)], [], _ep_sigmoid, 2 * d, F32, tm_big, 1024, "proj_gates")

        o_a_p = swa_prompt(proj, sinks[i], bp, seq)
        ps = proj[mp:]
        q_s = ps[:, :ATT_Q].reshape(bs, t_new, N_KV_HEADS, GROUP, HEAD_DIM).transpose(0, 2, 3, 1, 4)
        q_s = q_s.reshape(bs, N_KV_HEADS, GROUP * t_new, HEAD_DIM)
        k_s = ps[:, ATT_Q:ATT_Q + ATT_KV].reshape(bs, t_new, N_KV_HEADS, HEAD_DIM)
        v_s = ps[:, ATT_Q + ATT_KV:COL_QKV_D].reshape(bs, t_new, N_KV_HEADS, HEAD_DIM)
        o_a_s = swa_sample(q_s, k_s.transpose(0, 2, 1, 3), v_s.transpose(0, 2, 1, 3),
                           cache_k[i].transpose(0, 2, 1, 3), cache_v[i].transpose(0, 2, 1, 3), sinks[i])
        o_a_s = o_a_s.reshape(bs, N_KV_HEADS, GROUP, t_new, HEAD_DIM).transpose(0, 3, 1, 2, 4).reshape(ms, ATT_Q)
        o_a = (o_a_p, o_a_s.astype(BF16))
        tail = lambda n, c0, c1: jnp.stack(
            [lax.slice(proj, ((b + 1) * seq - n, c0), ((b + 1) * seq, c1)) for b in range(bp)])
        outs["kp"].append(tail(WINDOW, ATT_Q, ATT_Q + ATT_KV).reshape(bp, WINDOW, N_KV_HEADS, HEAD_DIM))
        outs["vp"].append(tail(WINDOW, ATT_Q + ATT_KV, COL_QKV_D).reshape(bp, WINDOW, N_KV_HEADS, HEAD_DIM))
        outs["ks"].append(jnp.concatenate([cache_k[i], k_s], axis=1)[:, -w_cache:])
        outs["vs"].append(jnp.concatenate([cache_v[i], v_s], axis=1)[:, -w_cache:])

        o_d_p, s_p = gdn_prompt(proj, ba, conv_w[i], a_log[i], dt_bias[i], dn_norm[i], bp, seq)
        qkv_s = ps[:, COL_QKV_D:COL_Z].reshape(bs, t_new, CONV_DIM)
        rows = SUBLANES
        pad_rows = rows - (CONV_W - 1) - t_new
        xp_s = jnp.concatenate([state_conv[i], qkv_s, jnp.zeros((bs, pad_rows, CONV_DIM), F32)], axis=1)
        tok_pad = lambda a: jnp.pad(a.reshape(bs, t_new, -1), ((0, 0), (0, rows - t_new), (0, 0)))
        o_d_s, s_s = gdn_sample(xp_s, tok_pad(ps[:, COL_Z:COL_BA]), tok_pad(ba[mp:]), conv_w[i], a_log[i],
                                dt_bias[i], dn_norm[i], state_ssm[i], t_new)
        o_d = (o_d_p, o_d_s[:, :t_new].reshape(ms, DN_V))
        outs["cp"].append(tail(CONV_W - 1, COL_QKV_D, COL_Z))
        outs["cs"].append(jnp.concatenate([state_conv[i], qkv_s], axis=1)[:, -(CONV_W - 1):])
        outs["sp"].append(s_p)
        outs["ss"].append(s_s)

        merged = fused_mm([(o_a, w_attn_branch[i], 0), (o_d, w_dn_branch[i], 0)], [gates, ColOff(gates, d)],
                          _ep_merge, d, BF16, tm_split, 1024, "branch_merge")
        x = fused_mm([(merged, w_out[i], 0)], [x], _ep_residual, d, F32, tm_split, 1024, "out_proj")

        x, hp = routed_experts_residual(x, norm_moe[i], w_router[i], b_router[i], w_exp_in[i], b_exp_in[i],
                                        w_exp_out[i], b_exp_out[i], norm_ple[i])

        x = fused_mm([(hp, w_ple_gate[i], 0), ((p_prompt[i], p_sample[i]), w_ple_proj[i], 0)], [x], _ep_ple, d, F32,
                     tm_split, 1024, "ple")
    y_p, y_s = rms_rows(x, norm_final, F32, tm_split, split_rows=(mp, ms))
    st = lambda k: jnp.stack(outs[k])
    return (y_p.reshape(bp, seq, d), y_s.reshape(bs, t_new, d), st("kp"), st("vp"), st("cp"), st("sp"),
            st("ks"), st("vs"), st("cs"), st("ss"))
```

```python
import collections
import functools

import jax
import jax.numpy as jnp
from jax import lax
from jax.experimental import pallas as pl
from jax.experimental.pallas import tpu as pltpu

F32 = jnp.float32
BF16 = jnp.bfloat16
I32 = jnp.int32

D_MODEL = 2048
N_Q_HEADS = 16
N_KV_HEADS = 4
GROUP = N_Q_HEADS // N_KV_HEADS
HEAD_DIM = 64
WINDOW = 128
N_DN_HEADS = 16
DK = 128
DV = 128
CONV_W = 4
CONV_DIM = N_DN_HEADS * (2 * DK + DV)
N_EXPERTS = 32
TOP_K = 4
D_FF = D_MODEL
SWIGLU_LIMIT = 7.0
SWIGLU_ALPHA = 1.702
EPS = 1e-6
ATT_Q = N_Q_HEADS * HEAD_DIM
ATT_KV = N_KV_HEADS * HEAD_DIM
DN_V = N_DN_HEADS * DV

COL_QKV_D = ATT_Q + 2 * ATT_KV
COL_Z = COL_QKV_D + CONV_DIM
COL_BA = COL_Z + DN_V
COL_GATES = COL_BA + 2 * N_DN_HEADS

LANES = 128
SUBLANES = 8
VMEM_LIMIT_BYTES = 56 * 2 ** 20

GDN_CHUNK = 64
GDN_TBLOCK = 512
GDN_HG = 4
GDN_SAMPLE_HEADS = 8
MOE_ROWS = 1024
MOE_SUB = 256
MOE_TF = 256
MOE_KC = 512
TOK_TILE = 256
NEG_BIG = -1e30


def _cp(sem, vmem=VMEM_LIMIT_BYTES):
    return pltpu.CompilerParams(dimension_semantics=sem, vmem_limit_bytes=vmem)


def _sigmoid(x):
    return 1.0 / (1.0 + jnp.exp(-x))


def _softplus(x):
    return jnp.maximum(x, 0.0) + jnp.log1p(jnp.exp(-jnp.abs(x)))


def _bdot(a, b):
    return jnp.dot(a.astype(BF16), b.astype(BF16), preferred_element_type=F32)


def _bdot_nt(a, b):
    return lax.dot_general(a.astype(BF16), b.astype(BF16), (((1,), (1,)), ((), ())),
                           preferred_element_type=F32)


def _bdot_tn(a, b):
    return lax.dot_general(a.astype(BF16), b.astype(BF16), (((0,), (0,)), ((), ())),
                           preferred_element_type=F32)


def _fdot(a, b):
    return jnp.dot(a, b, preferred_element_type=F32, precision=lax.Precision.HIGHEST)


ColOff = collections.namedtuple("ColOff", ["x", "off"])


def _parts(x):
    return tuple(x) if isinstance(x, (tuple, list)) else (x,)


def _tile_starts(parts, tm):
    starts, s = [], 0
    for p in parts:
        assert p.shape[0] % tm == 0
        starts.append(s)
        s += p.shape[0] // tm
    return tuple(starts), s


def _part_spec(part, start, tm, cols, col_of, tile_arg):
    n = part.shape[0] // tm

    def index_map(*g):
        return (jnp.clip(g[tile_arg] - start, 0, n - 1), col_of(*g))

    return pl.BlockSpec((tm, cols), index_map)


def _select_part(refs, starts, i):
    v = refs[0][...]
    for r, s in zip(refs[1:], starts[1:]):
        v = jnp.where(i >= s, r[...], v)
    return v


def _rms_kernel(*refs, in_starts, out_starts):
    i = pl.program_id(0)
    x_refs = refs[:len(in_starts)]
    g_ref = refs[len(in_starts)]
    o_refs = refs[len(in_starts) + 1:]
    x = _select_part(x_refs, in_starts, i)
    y = x * lax.rsqrt(jnp.mean(x * x, axis=-1, keepdims=True) + EPS) * g_ref[...]
    if len(o_refs) == 1:
        o_refs[0][...] = y.astype(o_refs[0].dtype)
        return
    bounds = list(out_starts[1:]) + [None]
    for o_ref, lo, hi in zip(o_refs, out_starts, bounds):
        cond = (i >= lo) if hi is None else ((i >= lo) & (i < hi))

        @pl.when(cond)
        def _():
            o_ref[...] = y.astype(o_ref.dtype)


def rms_rows(x, gamma, out_dtype, tm=512, split_rows=None):
    parts = _parts(x)
    d = parts[0].shape[1]
    in_starts, n_tiles = _tile_starts(parts, tm)
    in_specs = [_part_spec(p, st, tm, d, lambda i: 0, 0) for p, st in zip(parts, in_starts)]
    in_specs.append(pl.BlockSpec((1, d), lambda i: (0, 0)))
    if split_rows is None:
        out_starts = (0,)
        out_specs = pl.BlockSpec((tm, d), lambda i: (i, 0))
        out_shape = jax.ShapeDtypeStruct((n_tiles * tm, d), out_dtype)
    else:
        shapes = [jax.ShapeDtypeStruct((r, d), out_dtype) for r in split_rows]
        out_starts, total = _tile_starts(shapes, tm)
        assert total == n_tiles
        out_specs = [_part_spec(sh, st, tm, d, lambda i: 0, 0) for sh, st in zip(shapes, out_starts)]
        out_shape = shapes
    return pl.pallas_call(
        functools.partial(_rms_kernel, in_starts=in_starts, out_starts=tuple(out_starts)),
        grid=(n_tiles,),
        in_specs=in_specs,
        out_specs=out_specs,
        out_shape=out_shape,
        compiler_params=_cp(("arbitrary",)),
        name="rms_rows",
    )(*parts, gamma.reshape(1, d))


def _fused_mm_kernel(*refs, a_starts, e_starts, epilogue):
    i = pl.program_id(1)
    pos = 0
    a_groups = []
    for st in a_starts:
        a_groups.append(refs[pos:pos + len(st)])
        pos += len(st)
    w_refs = refs[pos:pos + len(a_starts)]
    pos += len(a_starts)
    e_groups = []
    for st in e_starts:
        e_groups.append(refs[pos:pos + len(st)])
        pos += len(st)
    o_ref = refs[pos]
    wb_refs = refs[pos + 1:]

    @pl.when(i == 0)
    def _():
        for w_ref, wb_ref in zip(w_refs, wb_refs):
            wb_ref[...] = w_ref[...].astype(BF16)

    dots = [jnp.dot(_select_part(g, st, i).astype(BF16), wb_ref[...], preferred_element_type=F32)
            for g, st, wb_ref in zip(a_groups, a_starts, wb_refs)]
    extras = [_select_part(g, st, i) for g, st in zip(e_groups, e_starts)]
    o_ref[...] = epilogue(dots, extras).astype(o_ref.dtype)


def fused_mm(pairs, extras, epilogue, n_out, out_dtype, tm, tn, name):
    in_specs, args, scratch, a_starts, e_starts = [], [], [], [], []
    n_tiles = None
    for a, _, _ in pairs:
        parts = _parts(a)
        starts, total = _tile_starts(parts, tm)
        n_tiles = total if n_tiles is None else n_tiles
        assert total == n_tiles
        a_starts.append(starts)
        for p, st in zip(parts, starts):
            in_specs.append(_part_spec(p, st, tm, p.shape[1], lambda j, i: 0, 1))
            args.append(p)
    for a, w, off in pairs:
        k = _parts(a)[0].shape[1]
        assert off % tn == 0 and w.shape[0] == k and n_out % tn == 0
        in_specs.append(pl.BlockSpec((k, tn), functools.partial(lambda j, i, o: (0, j + o), o=off // tn)))
        args.append(w)
        scratch.append(pltpu.VMEM((k, tn), BF16))
    for e in extras:
        e, off = (e.x, e.off) if isinstance(e, ColOff) else (e, 0)
        assert off % tn == 0
        parts = _parts(e)
        starts, total = _tile_starts(parts, tm)
        assert total == n_tiles
        e_starts.append(starts)
        for p, st in zip(parts, starts):
            in_specs.append(_part_spec(p, st, tm, tn, functools.partial(lambda j, i, o: j + o, o=off // tn), 1))
            args.append(p)
    kern = functools.partial(_fused_mm_kernel, a_starts=tuple(a_starts), e_starts=tuple(e_starts),
                             epilogue=epilogue)
    return pl.pallas_call(
        kern,
        grid=(n_out // tn, n_tiles),
        in_specs=in_specs,
        out_specs=pl.BlockSpec((tm, tn), lambda j, i: (i, j)),
        out_shape=jax.ShapeDtypeStruct((n_tiles * tm, n_out), out_dtype),
        scratch_shapes=scratch,
        compiler_params=_cp(("arbitrary", "arbitrary")),
        name=name,
    )(*args)


def _swa_prompt_kernel(sink_ref, q_ref, kp_ref, kc_ref, vp_ref, vc_ref, o_ref):
    j = pl.program_id(1)
    w = WINDOW
    qi = lax.broadcasted_iota(I32, (w, 2 * w), 0)
    km = lax.broadcasted_iota(I32, (w, 2 * w), 1)
    band = (km >= qi) & (km <= w + qi) & ((j > 0) | (km >= w))
    mask4 = jnp.concatenate([band] * GROUP, axis=0)
    lane = lax.broadcasted_iota(I32, (1, LANES), 1)
    lo = lane < HEAD_DIM
    scale = HEAD_DIM ** -0.5
    heads = range(N_KV_HEADS)

    def both_halves(prev_ref, cur_ref, h):
        pair = (h // 2) * LANES
        x = jnp.concatenate([prev_ref[:, pair:pair + LANES], cur_ref[:, pair:pair + LANES]], axis=0)
        x = jnp.where(lo if h % 2 == 0 else ~lo, x, 0.0)
        return x + pltpu.roll(x, HEAD_DIM, axis=1)

    def q_group(h):
        qs = []
        for g in range(GROUP):
            n = h * GROUP + g
            qp = q_ref[:, (n // 2) * LANES:(n // 2 + 1) * LANES]
            qs.append(jnp.where(lo if n % 2 == 0 else ~lo, qp, 0.0))
        return jnp.concatenate(qs, axis=0)

    k2 = [both_halves(kp_ref, kc_ref, h) for h in heads]
    v2 = [both_halves(vp_ref, vc_ref, h) for h in heads]
    q4 = [q_group(h) for h in heads]
    sink = [jnp.concatenate([jnp.full((w, 1), sink_ref[h * GROUP + g], F32) for g in range(GROUP)], axis=0)
            for h in heads]
    s = [jnp.where(mask4, _bdot_nt(q, k) * scale, NEG_BIG) for q, k in zip(q4, k2)]
    m = [jnp.maximum(jnp.max(x, axis=-1, keepdims=True), sk) for x, sk in zip(s, sink)]
    p = [jnp.exp(x - mm) for x, mm in zip(s, m)]
    den = [jnp.sum(x, axis=-1, keepdims=True) + jnp.exp(sk - mm) for x, sk, mm in zip(p, sink, m)]
    o4 = [_bdot(x, v) / dd for x, v, dd in zip(p, v2, den)]
    for h in heads:
        for g in range(0, GROUP, 2):
            n = h * GROUP + g
            pair_out = jnp.where(lo, o4[h][g * w:(g + 1) * w], o4[h][(g + 1) * w:(g + 2) * w])
            o_ref[:, (n // 2) * LANES:(n // 2 + 1) * LANES] = pair_out.astype(o_ref.dtype)


def swa_prompt(proj, sinks, batch, seq):
    nb = seq // WINDOW
    row = lambda b, j: b * nb + j
    return pl.pallas_call(
        _swa_prompt_kernel,
        grid=(batch, nb),
        in_specs=[
            pl.BlockSpec(memory_space=pltpu.SMEM),
            pl.BlockSpec((WINDOW, ATT_Q), lambda b, j: (row(b, j), 0)),
            pl.BlockSpec((WINDOW, ATT_KV), lambda b, j: (row(b, jnp.maximum(j - 1, 0)), ATT_Q // ATT_KV)),
            pl.BlockSpec((WINDOW, ATT_KV), lambda b, j: (row(b, j), ATT_Q // ATT_KV)),
            pl.BlockSpec((WINDOW, ATT_KV), lambda b, j: (row(b, jnp.maximum(j - 1, 0)), ATT_Q // ATT_KV + 1)),
            pl.BlockSpec((WINDOW, ATT_KV), lambda b, j: (row(b, j), ATT_Q // ATT_KV + 1)),
        ],
        out_specs=pl.BlockSpec((WINDOW, ATT_Q), lambda b, j: (row(b, j), 0)),
        out_shape=jax.ShapeDtypeStruct((batch * seq, ATT_Q), BF16),
        compiler_params=_cp(("parallel", "arbitrary")),
        name="swa_prompt",
    )(sinks, proj, proj, proj, proj, proj)


def _swa_sample_kernel(sink_ref, q_ref, kn_ref, vn_ref, ck_ref, cv_ref, o_ref, *, t_new, w_cache):
    bg = q_ref.shape[0]
    nb = bg * N_KV_HEADS
    rows = GROUP * t_new
    q = q_ref[...].reshape(nb, rows, HEAD_DIM)
    kn = kn_ref[...].reshape(nb, t_new, HEAD_DIM)
    vn = vn_ref[...].reshape(nb, t_new, HEAD_DIM)
    ck = ck_ref[...].reshape(nb, w_cache, HEAD_DIM)
    cv = cv_ref[...].reshape(nb, w_cache, HEAD_DIM)
    scale = HEAD_DIM ** -0.5
    bdims = (((2,), (2,)), ((0,), (0,)))
    s_c = lax.dot_general(q.astype(BF16), ck.astype(BF16), bdims, preferred_element_type=F32) * scale
    s_n = lax.dot_general(q.astype(BF16), kn.astype(BF16), bdims, preferred_element_type=F32) * scale
    qi_c = lax.broadcasted_iota(I32, (nb, rows, w_cache), 1) % t_new
    kc = lax.broadcasted_iota(I32, (nb, rows, w_cache), 2)
    s_c = jnp.where(kc >= w_cache + qi_c - WINDOW, s_c, NEG_BIG)
    qi_n = lax.broadcasted_iota(I32, (nb, rows, t_new), 1) % t_new
    kk = lax.broadcasted_iota(I32, (nb, rows, t_new), 2)
    s_n = jnp.where(kk <= qi_n, s_n, NEG_BIG)
    bi = lax.broadcasted_iota(I32, (nb, rows, 1), 0) % N_KV_HEADS
    gi = lax.broadcasted_iota(I32, (nb, rows, 1), 1) // t_new
    head = bi * GROUP + gi
    sink = jnp.zeros((nb, rows, 1), F32)
    for n in range(N_Q_HEADS):
        sink = jnp.where(head == n, sink_ref[n], sink)
    m = jnp.maximum(jnp.maximum(jnp.max(s_c, axis=-1, keepdims=True), jnp.max(s_n, axis=-1, keepdims=True)), sink)
    p_c = jnp.exp(s_c - m)
    p_n = jnp.exp(s_n - m)
    den = jnp.sum(p_c, axis=-1, keepdims=True) + jnp.sum(p_n, axis=-1, keepdims=True) + jnp.exp(sink - m)
    pv = (((2,), (1,)), ((0,), (0,)))
    o = (lax.dot_general(p_c.astype(BF16), cv.astype(BF16), pv, preferred_element_type=F32)
         + lax.dot_general(p_n.astype(BF16), vn.astype(BF16), pv, preferred_element_type=F32)) / den
    o_ref[...] = o.reshape(bg, N_KV_HEADS, rows, HEAD_DIM).astype(o_ref.dtype)


def swa_sample(q, kn, vn, ck, cv, sinks, bg=8):
    b, _, rows, _ = q.shape
    t_new, w_cache = kn.shape[2], ck.shape[2]
    spec = lambda r: pl.BlockSpec((bg, N_KV_HEADS, r, HEAD_DIM), lambda i: (i, 0, 0, 0))
    return pl.pallas_call(
        functools.partial(_swa_sample_kernel, t_new=t_new, w_cache=w_cache),
        grid=(b // bg,),
        in_specs=[pl.BlockSpec(memory_space=pltpu.SMEM), spec(rows), spec(t_new), spec(t_new),
                  spec(w_cache), spec(w_cache)],
        out_specs=spec(rows),
        out_shape=jax.ShapeDtypeStruct(q.shape, F32),
        compiler_params=_cp(("parallel",)),
        name="swa_sample",
    )(sinks, q, kn, vn, ck, cv)


def _unit_lower_inverses(a_list):
    c = a_list[0].shape[0]
    ri = lax.broadcasted_iota(I32, (c, c), 0)
    ci = lax.broadcasted_iota(I32, (c, c), 1)
    eye = jnp.where(ri == ci, 1.0, 0.0)
    ps = [-a for a in a_list]
    ts = [eye + p for p in ps]
    span = 2
    while span < c:
        ps = [_bdot(p, p) for p in ps]
        ts = [t + _bdot(t, p) for t, p in zip(ts, ps)]
        span *= 2
    return ts


def _conv_silu(ext_ref, x_ref, cw_ref, tb):
    ext_ref[SUBLANES:tb + SUBLANES, :] = x_ref[...]
    y = None
    for jj in range(CONV_W):
        term = ext_ref[SUBLANES - (CONV_W - 1) + jj:SUBLANES - (CONV_W - 1) + jj + tb, :] * cw_ref[jj:jj + 1, :]
        y = term if y is None else y + term
    ext_ref[0:SUBLANES, :] = ext_ref[tb:tb + SUBLANES, :]
    return y * _sigmoid(y)


def _l2norm(x):
    return x * lax.rsqrt(jnp.sum(x * x, axis=-1, keepdims=True) + EPS)


def _pad_a_lanes(v):
    return jnp.zeros((1, LANES), F32).at[0, N_DN_HEADS:2 * N_DN_HEADS].set(v)


def _gdn_prompt_kernel(q_ref, k_ref, v_ref, z_ref, ba_ref, cwq_ref, cwk_ref, cwv_ref, alog_ref, dtb_ref,
                       dnw_ref, o_ref, sout_ref, extq, extk, extv, s_scr, *, tb):
    hg = pl.program_id(1)
    t = pl.program_id(2)
    c = GDN_CHUNK
    nch = tb // c

    @pl.when(t == 0)
    def _():
        zero = jnp.zeros((SUBLANES, extq.shape[1]), F32)
        extq[0:SUBLANES, :] = zero
        extk[0:SUBLANES, :] = zero
        extv[0:SUBLANES, :] = zero
        s_scr[...] = jnp.zeros(s_scr.shape, F32)

    qc = _conv_silu(extq, q_ref, cwq_ref, tb)
    kc = _conv_silu(extk, k_ref, cwk_ref, tb)
    vc = _conv_silu(extv, v_ref, cwv_ref, tb)
    ba = ba_ref[...]
    lane = lax.broadcasted_iota(I32, (tb, LANES), 1)
    beta_all = _sigmoid(ba)
    g_all = -jnp.exp(alog_ref[...]) * _softplus(ba + dtb_ref[...])
    rt = lax.broadcasted_iota(I32, (tb, tb), 0)
    ct = lax.broadcasted_iota(I32, (tb, tb), 1)
    shift = c.bit_length() - 1
    same = jnp.right_shift(rt, shift) == jnp.right_shift(ct, shift)
    gcum_all = _fdot(jnp.where(same & (rt >= ct), 1.0, 0.0), g_all)
    ri = lax.broadcasted_iota(I32, (c, c), 0)
    ci = lax.broadcasted_iota(I32, (c, c), 1)
    incl = ri >= ci
    strict = ri > ci

    units = [(hh, ch) for hh in range(GDN_HG) for ch in range(nch)]
    qn_h, kn_h, v_h, beta_h, gc_h, grow_h = [], [], [], [], [], []
    for hh in range(GDN_HG):
        h = hg * GDN_HG + hh
        sl = slice(hh * DK, (hh + 1) * DK)
        qn_h.append(_l2norm(qc[:, sl]) * (DK ** -0.5))
        kn_h.append(_l2norm(kc[:, sl]))
        v_h.append(vc[:, sl])
        beta_h.append(jnp.sum(jnp.where(lane == h, beta_all, 0.0), axis=-1, keepdims=True))
        gcol = jnp.sum(jnp.where(lane == N_DN_HEADS + h, gcum_all, 0.0), axis=-1, keepdims=True)
        gb = jnp.broadcast_to(gcol, (tb, LANES))
        gc_h.append(gb)
        grow_h.append(gb.T)
    rows = lambda ch: slice(ch * c, (ch + 1) * c)
    qn = [qn_h[hh][rows(ch)] for hh, ch in units]
    kn = [kn_h[hh][rows(ch)] for hh, ch in units]
    vv = [v_h[hh][rows(ch)] for hh, ch in units]
    beta = [beta_h[hh][rows(ch)] for hh, ch in units]
    gc = [gc_h[hh][rows(ch)] for hh, ch in units]
    grow = [grow_h[hh][0:c, ch * c:(ch + 1) * c] for hh, ch in units]
    eg = [jnp.exp(g) for g in gc]
    kb = [k * b for k, b in zip(kn, beta)]
    vb = [v * b for v, b in zip(vv, beta)]
    kexp = [k * e for k, e in zip(kb, eg)]
    qexp = [q * e for q, e in zip(qn, eg)]
    glast = [g[c - 1:c, :] for g in gc]
    kend = [k * jnp.exp(gl - g) for k, gl, g in zip(kn, glast, gc)]
    sdec = [jnp.exp(gl) for gl in glast]
    ak = [_bdot_nt(jnp.concatenate([a, q], axis=0), k) for a, q, k in zip(kb, qn, kn)]
    decay = [jnp.where(incl, jnp.exp(jnp.where(incl, g[:, :c] - gr, 0.0)), 0.0) for g, gr in zip(gc, grow)]
    a_mat = [jnp.where(strict, x[:c] * d, 0.0) for x, d in zip(ak, decay)]
    qk = [jnp.where(incl, x[c:] * d, 0.0) for x, d in zip(ak, decay)]
    t_inv = _unit_lower_inverses(a_mat)
    uw = [_bdot(ti, jnp.concatenate([v, k], axis=1)) for ti, v, k in zip(t_inv, vb, kexp)]

    s = [s_scr[hh] for hh in range(GDN_HG)]
    outs = [[] for _ in range(GDN_HG)]
    for ch in range(nch):
        idx = [hh * nch + ch for hh in range(GDN_HG)]
        ws = [_bdot(jnp.concatenate([uw[i][:, DV:], qexp[i]], axis=0), s[hh]) for hh, i in enumerate(idx)]
        u = [uw[i][:, :DV] - w[:c] for i, w in zip(idx, ws)]
        o = [w[c:] + _bdot(qk[i], uu) for i, w, uu in zip(idx, ws, u)]
        s = [s[hh] * sdec[i] + _bdot_tn(kend[i], uu) for (hh, i), uu in zip(enumerate(idx), u)]
        for hh in range(GDN_HG):
            outs[hh].append(o[hh])
    dnw = dnw_ref[...]
    for hh in range(GDN_HG):
        s_scr[hh] = s[hh]
        sl = slice(hh * DK, (hh + 1) * DK)
        o = jnp.concatenate(outs[hh], axis=0)
        o = o * lax.rsqrt(jnp.mean(o * o, axis=-1, keepdims=True) + EPS) * dnw
        zh = z_ref[:, sl]
        o_ref[:, sl] = (o * (zh * _sigmoid(zh))).astype(o_ref.dtype)

    @pl.when(t == pl.num_programs(2) - 1)
    def _():
        sout_ref[0] = s_scr[...]


def gdn_prompt(proj, ba, conv_w, a_log, dt_bias, dn_norm, batch, seq):
    tb = GDN_TBLOCK
    nt = seq // tb
    wblk = GDN_HG * DK
    q0, k0, v0, z0 = (COL_QKV_D // wblk, (COL_QKV_D + N_DN_HEADS * DK) // wblk,
                      (COL_QKV_D + 2 * N_DN_HEADS * DK) // wblk, COL_Z // wblk)
    nhg = N_DN_HEADS // GDN_HG
    row = lambda b, t: b * nt + t
    col_spec = lambda c0: pl.BlockSpec((tb, wblk), lambda b, hg, t: (row(b, t), c0 + hg))
    cw_spec = lambda c0: pl.BlockSpec((CONV_W, wblk), lambda b, hg, t: (0, c0 + hg))
    vec = pl.BlockSpec((1, LANES), lambda b, hg, t: (0, 0))
    return pl.pallas_call(
        functools.partial(_gdn_prompt_kernel, tb=tb),
        grid=(batch, nhg, nt),
        in_specs=[col_spec(q0), col_spec(k0), col_spec(v0), col_spec(z0),
                  pl.BlockSpec((tb, LANES), lambda b, hg, t: (row(b, t), 0)),
                  cw_spec(0), cw_spec(nhg), cw_spec(2 * nhg), vec, vec, vec],
        out_specs=[pl.BlockSpec((tb, wblk), lambda b, hg, t: (row(b, t), hg)),
                   pl.BlockSpec((1, GDN_HG, DK, DV), lambda b, hg, t: (b, hg, 0, 0))],
        out_shape=[jax.ShapeDtypeStruct((batch * seq, DN_V), BF16),
                   jax.ShapeDtypeStruct((batch, N_DN_HEADS, DK, DV), F32)],
        scratch_shapes=[pltpu.VMEM((tb + SUBLANES, wblk), F32)] * 3 + [pltpu.VMEM((GDN_HG, DK, DV), F32)],
        compiler_params=_cp(("parallel", "parallel", "arbitrary")),
        name="gdn_prompt",
    )(proj, proj, proj, proj, ba, conv_w, conv_w, conv_w, _pad_a_lanes(a_log), _pad_a_lanes(dt_bias),
      dn_norm.reshape(1, DV))


def _gdn_sample_kernel(xp_ref, z_ref, ba_ref, cw_ref, alog_ref, dtb_ref, dnw_ref, s0_ref, o_ref, s_ref, *, t_new):
    bg = xp_ref.shape[0]
    rows = xp_ref.shape[1]
    ri = lax.broadcasted_iota(I32, (rows, LANES), 0)
    vmask = ri < t_new
    r8 = lax.broadcasted_iota(I32, (rows, rows), 0)
    c8 = lax.broadcasted_iota(I32, (rows, rows), 1)
    incl = r8 >= c8
    strict = r8 > c8
    ltri = jnp.where(incl, 1.0, 0.0)
    ones8 = jnp.ones((rows, rows), F32)
    dnw = dnw_ref[...]

    def per_batch(b, carry):
        xp = xp_ref[b]
        y = None
        for jj in range(CONV_W):
            sh = xp if jj == 0 else pltpu.roll(xp, rows - jj, axis=0)
            term = sh * cw_ref[jj:jj + 1, :]
            y = term if y is None else y + term
        y = y * _sigmoid(y)
        ba = ba_ref[b]
        beta_all = jnp.where(vmask, _sigmoid(ba), 0.0)
        g_all = jnp.where(vmask, -jnp.exp(alog_ref[...]) * _softplus(ba + dtb_ref[...]), 0.0)
        gcum_all = _fdot(ltri, g_all)
        z = z_ref[b]
        for h0 in range(0, N_DN_HEADS, GDN_SAMPLE_HEADS):
            hs = list(range(h0, h0 + GDN_SAMPLE_HEADS))
            qn = [_l2norm(y[:, h * DK:(h + 1) * DK]) * (DK ** -0.5) for h in hs]
            kn = [jnp.where(vmask, _l2norm(y[:, (N_DN_HEADS + h) * DK:(N_DN_HEADS + h + 1) * DK]), 0.0)
                  for h in hs]
            vh = [y[:, (2 * N_DN_HEADS + h) * DK:(2 * N_DN_HEADS + h + 1) * DK] for h in hs]
            beta = [beta_all[:, h:h + 1] for h in hs]
            gc = [jnp.broadcast_to(gcum_all[:, N_DN_HEADS + h:N_DN_HEADS + h + 1], (rows, LANES)) for h in hs]
            grow = [_fdot(ones8, jnp.where(r8 == c8, g[:, :rows], 0.0)) for g in gc]
            decay = [jnp.where(incl, jnp.exp(jnp.where(incl, g[:, :rows] - gr, 0.0)), 0.0)
                     for g, gr in zip(gc, grow)]
            eg = [jnp.exp(g) for g in gc]
            kb = [k * bt for k, bt in zip(kn, beta)]
            vb = [v * bt for v, bt in zip(vh, beta)]
            kexp = [k * e for k, e in zip(kb, eg)]
            qexp = [q * e for q, e in zip(qn, eg)]
            glast = [g[rows - 1:rows, :] for g in gc]
            kend = [k * jnp.exp(gl - g) for k, gl, g in zip(kn, glast, gc)]
            sdec = [jnp.exp(gl) for gl in glast]
            ak = [_bdot_nt(jnp.concatenate([a, q], axis=0), k) for a, q, k in zip(kb, qn, kn)]
            a_mat = [jnp.where(strict, x[:rows] * d, 0.0) for x, d in zip(ak, decay)]
            qk = [jnp.where(incl, x[rows:] * d, 0.0) for x, d in zip(ak, decay)]
            t_inv = _unit_lower_inverses(a_mat)
            uw = [_bdot(ti, jnp.concatenate([v, k], axis=1)) for ti, v, k in zip(t_inv, vb, kexp)]
            s = [s0_ref[b, h] for h in hs]
            ws = [_bdot(jnp.concatenate([x[:, DV:], q], axis=0), a) for x, q, a in zip(uw, qexp, s)]
            u = [x[:, :DV] - w[:rows] for x, w in zip(uw, ws)]
            o_rows = [w[rows:] + _bdot(m, uu) for w, m, uu in zip(ws, qk, u)]
            s = [a * sd + _bdot_tn(k, uu) for a, sd, k, uu in zip(s, sdec, kend, u)]
            for h, a, orow in zip(hs, s, o_rows):
                s_ref[b, h] = a
                o = orow * lax.rsqrt(jnp.mean(orow * orow, axis=-1, keepdims=True) + EPS) * dnw
                zh = z[:, h * DV:(h + 1) * DV]
                o_ref[b, :, h * DV:(h + 1) * DV] = (o * (zh * _sigmoid(zh))).astype(o_ref.dtype)
        return carry

    lax.fori_loop(0, bg, per_batch, 0)


def gdn_sample(xp, z, ba, conv_w, a_log, dt_bias, dn_norm, s0, t_new, bg=4):
    b = xp.shape[0]
    rows = xp.shape[1]
    full = lambda shape: pl.BlockSpec(shape, lambda i: (0,) * len(shape))
    return pl.pallas_call(
        functools.partial(_gdn_sample_kernel, t_new=t_new),
        grid=(b // bg,),
        in_specs=[pl.BlockSpec((bg, rows, CONV_DIM), lambda i: (i, 0, 0)),
                  pl.BlockSpec((bg, rows, DN_V), lambda i: (i, 0, 0)),
                  pl.BlockSpec((bg, rows, LANES), lambda i: (i, 0, 0)),
                  full((CONV_W, CONV_DIM)), full((1, LANES)), full((1, LANES)), full((1, DV)),
                  pl.BlockSpec((bg, N_DN_HEADS, DK, DV), lambda i: (i, 0, 0, 0))],
        out_specs=[pl.BlockSpec((bg, rows, DN_V), lambda i: (i, 0, 0)),
                   pl.BlockSpec((bg, N_DN_HEADS, DK, DV), lambda i: (i, 0, 0, 0))],
        out_shape=[jax.ShapeDtypeStruct((b, rows, DN_V), BF16),
                   jax.ShapeDtypeStruct((b, N_DN_HEADS, DK, DV), F32)],
        compiler_params=_cp(("parallel",)),
        name="gdn_sample",
    )(xp, z, ba, conv_w, _pad_a_lanes(a_log), _pad_a_lanes(dt_bias), dn_norm.reshape(1, DV), s0)


def _router_kernel(x_ref, g_ref, wr_ref, br_ref, mi_ref, mf_ref, cnt_ref, run_ref):
    i = pl.program_id(0)
    tm = x_ref.shape[0]

    @pl.when(i == 0)
    def _():
        run_ref[...] = jnp.zeros(run_ref.shape, F32)

    x = x_ref[...]
    xn = x * lax.rsqrt(jnp.mean(x * x, axis=-1, keepdims=True) + EPS) * g_ref[...]
    lane = lax.broadcasted_iota(I32, (tm, LANES), 1)
    lane_f = lane.astype(F32)
    logits = _fdot(xn, wr_ref[...]) + br_ref[...]
    work = jnp.where(lane < N_EXPERTS, logits, -jnp.inf)
    vals, idxs, hots = [], [], []
    for _ in range(TOP_K):
        mx = jnp.max(work, axis=-1, keepdims=True)
        idx_f = jnp.min(jnp.where(work == mx, lane_f, float(LANES)), axis=-1, keepdims=True)
        idx = idx_f.astype(I32)
        hot = lane == idx
        work = jnp.where(hot, -jnp.inf, work)
        vals.append(mx)
        idxs.append(idx)
        hots.append(hot)
    exps = [jnp.exp(v - vals[0]) for v in vals]
    den = exps[0] + exps[1] + exps[2] + exps[3]
    member = jnp.where(hots[0] | hots[1] | hots[2] | hots[3], 1.0, 0.0)
    ri = lax.broadcasted_iota(I32, (tm, tm), 0)
    ci = lax.broadcasted_iota(I32, (tm, tm), 1)
    before = _bdot(jnp.where(ri > ci, 1.0, 0.0), member) + run_ref[0:1, :]
    run_ref[...] = run_ref[...] + jnp.sum(member, axis=0, keepdims=True)
    mi = jnp.zeros((tm, LANES), I32)
    mf = jnp.zeros((tm, LANES), F32)
    for k in range(TOP_K):
        rank = jnp.sum(jnp.where(hots[k], before, 0.0), axis=-1, keepdims=True).astype(I32)
        mi = jnp.where(lane == k, idxs[k], mi)
        mi = jnp.where(lane == TOP_K + k, rank, mi)
        mf = jnp.where(lane == k, exps[k] / den, mf)
    mi_ref[...] = mi
    mf_ref[...] = mf
    cnt_ref[...] = run_ref[...]


def router(x, gamma, w_router, b_router):
    m, d = x.shape
    tm = TOK_TILE
    wr = jnp.zeros((d, LANES), F32).at[:, :N_EXPERTS].set(w_router)
    br = jnp.zeros((1, LANES), F32).at[0, :N_EXPERTS].set(b_router)
    return pl.pallas_call(
        _router_kernel,
        grid=(m // tm,),
        in_specs=[pl.BlockSpec((tm, d), lambda i: (i, 0)), pl.BlockSpec((1, d), lambda i: (0, 0)),
                  pl.BlockSpec((d, LANES), lambda i: (0, 0)), pl.BlockSpec((1, LANES), lambda i: (0, 0))],
        out_specs=[pl.BlockSpec((tm, LANES), lambda i: (i, 0)), pl.BlockSpec((tm, LANES), lambda i: (i, 0)),
                   pl.BlockSpec((SUBLANES, LANES), lambda i: (0, 0))],
        out_shape=[jax.ShapeDtypeStruct((m, LANES), I32), jax.ShapeDtypeStruct((m, LANES), F32),
                   jax.ShapeDtypeStruct((SUBLANES, LANES), F32)],
        scratch_shapes=[pltpu.VMEM((SUBLANES, LANES), F32)],
        compiler_params=_cp(("arbitrary",)),
        name="router",
    )(x, gamma.reshape(1, d), wr, br)


def _zero_kernel(sub_ref, o_ref):
    del sub_ref
    o_ref[...] = jnp.zeros(o_ref.shape, o_ref.dtype)


def zero_partial_subblocks(sub_idx, n_rows, d):
    return pl.pallas_call(
        _zero_kernel,
        grid_spec=pltpu.PrefetchScalarGridSpec(
            num_scalar_prefetch=1, grid=(sub_idx.shape[0],), in_specs=[],
            out_specs=pl.BlockSpec((MOE_SUB, d), lambda e, sub: (sub[e], 0))),
        out_shape=jax.ShapeDtypeStruct((n_rows, d), F32),
        compiler_params=_cp(("arbitrary",)),
        name="moe_zero_tails",
    )(sub_idx)


def _row_copy(src_ref, src_row, dst_ref, dst_row, sem):
    return pltpu.make_async_copy(src_ref.at[pl.ds(src_row, 1), :], dst_ref.at[pl.ds(dst_row, 1), :], sem)


def _load_dest(dest_ref, dest_smem, sem):
    cp = pltpu.make_async_copy(dest_ref, dest_smem, sem)
    cp.start()
    cp.wait()


def _for_each_assignment(dest_smem, fn):
    per_row = LANES // TOP_K

    def body(g, c):
        for j in range(per_row):
            for k in range(TOP_K):
                fn(g * per_row + j, k, dest_smem[g, j * TOP_K + k])
        return c

    lax.fori_loop(0, dest_smem.shape[0], body, 0)


def _drain_rows(copy, dest_smem):
    def body(g, c):
        for _ in range(LANES):
            copy.wait()
        return c

    lax.fori_loop(0, dest_smem.shape[0], body, 0)


def _dispatch_kernel(dest_ref, x_ref, g_ref, xs_in_ref, xs_ref, xn_ref, dest_smem, sem_idx, sem):
    del xs_in_ref
    _load_dest(dest_ref, dest_smem, sem_idx)
    x = x_ref[...]
    xn_ref[...] = x * lax.rsqrt(jnp.mean(x * x, axis=-1, keepdims=True) + EPS) * g_ref[...]
    _for_each_assignment(
        dest_smem, lambda r, k, dst: _row_copy(xn_ref, r, xs_ref, dst, sem).start(priority=k % 2))
    _drain_rows(_row_copy(xn_ref, 0, xs_ref, 0, sem), dest_smem)


def dispatch(x, gamma, dest2d, xs):
    m, d = x.shape
    tm = TOK_TILE
    drows = tm * TOP_K // LANES
    return pl.pallas_call(
        _dispatch_kernel,
        grid=(m // tm,),
        in_specs=[pl.BlockSpec((drows, LANES), lambda i: (i, 0)),
                  pl.BlockSpec((tm, d), lambda i: (i, 0)),
                  pl.BlockSpec((1, d), lambda i: (0, 0)),
                  pl.BlockSpec(memory_space=pl.ANY)],
        out_specs=pl.BlockSpec(memory_space=pl.ANY),
        out_shape=jax.ShapeDtypeStruct(xs.shape, xs.dtype),
        scratch_shapes=[pltpu.VMEM((tm, d), F32), pltpu.SMEM((drows, LANES), I32), pltpu.SemaphoreType.DMA,
                        pltpu.SemaphoreType.DMA],
        input_output_aliases={3: 0},
        compiler_params=_cp(("arbitrary",)),
        name="moe_dispatch",
    )(dest2d, x, gamma.reshape(1, d), xs)


def _expert_kernel(blk_ref, bexp_ref, nsub_ref, last_ref, x0_ref, x1_ref, x2_ref, x3_ref, wg_ref, wu_ref,
                   bg_ref, bu_ref, wo_ref, bo_ref, o_ref, xn_ref, wgb_ref, wub_ref, wob_ref):
    del blk_ref, bexp_ref, last_ref
    b = pl.program_id(0)
    f = pl.program_id(1)
    n = nsub_ref[b]
    x_refs = (x0_ref, x1_ref, x2_ref, x3_ref)

    @pl.when(f == 0)
    def _():
        for s, x_ref in enumerate(x_refs):
            rows = slice(s * MOE_SUB, (s + 1) * MOE_SUB)

            @pl.when(s < n)
            def _():
                xn_ref[rows, :] = x_ref[...].astype(BF16)
                o_ref[rows, :] = jnp.broadcast_to(bo_ref[...], (MOE_SUB, o_ref.shape[1]))

            @pl.when((n > 0) & (s >= n))
            def _():
                o_ref[rows, :] = jnp.zeros((MOE_SUB, o_ref.shape[1]), F32)

    @pl.when((n > 0) & (n < MOE_ROWS // MOE_SUB))
    def _():
        wgb_ref[...] = wg_ref[...].astype(BF16)
        wub_ref[...] = wu_ref[...].astype(BF16)
        wob_ref[...] = wo_ref[...].astype(BF16)

    def run(subs):
        rows = [pl.ds(pl.multiple_of(s * MOE_SUB, MOE_SUB), MOE_SUB) for s in subs]
        xs = [xn_ref[r, :] for r in rows]
        gate = [jnp.dot(x, wgb_ref[...], preferred_element_type=F32) + bg_ref[...] for x in xs]
        up = [jnp.dot(x, wub_ref[...], preferred_element_type=F32) + bu_ref[...] for x in xs]
        gate = [jnp.minimum(g, SWIGLU_LIMIT) for g in gate]
        up = [jnp.clip(u, -SWIGLU_LIMIT, SWIGLU_LIMIT) for u in up]
        act = [((u + 1.0) * g * _sigmoid(SWIGLU_ALPHA * g)).astype(BF16) for u, g in zip(up, gate)]
        for r, a in zip(rows, act):
            o_ref[r, :] += jnp.dot(a, wob_ref[...], preferred_element_type=F32)

    full = MOE_ROWS // MOE_SUB

    @pl.when(n == full)
    def _():
        gate = bg_ref[...]
        up = bu_ref[...]
        for k0 in range(0, xn_ref.shape[1], MOE_KC):
            xk = xn_ref[:, k0:k0 + MOE_KC]
            gate = gate + jnp.dot(xk, wg_ref[k0:k0 + MOE_KC, :].astype(BF16), preferred_element_type=F32)
            up = up + jnp.dot(xk, wu_ref[k0:k0 + MOE_KC, :].astype(BF16), preferred_element_type=F32)
        gate = jnp.minimum(gate, SWIGLU_LIMIT)
        up = jnp.clip(up, -SWIGLU_LIMIT, SWIGLU_LIMIT)
        act = ((up + 1.0) * gate * _sigmoid(SWIGLU_ALPHA * gate)).astype(BF16)
        o_ref[...] += jnp.dot(act, wo_ref[...].astype(BF16), preferred_element_type=F32)

    @pl.when(n < full)
    def _():
        def pair(p, c):
            run([2 * p, 2 * p + 1])
            return c

        lax.fori_loop(0, lax.shift_right_logical(n, 1), pair, 0)

        @pl.when(lax.bitwise_and(n, 1) == 1)
        def _():
            run([n - 1])


def experts(xs, w_in, b_in, w_out, b_out, blk, bexp, nsub, last_sub):
    n_rows, d = xs.shape
    nb = blk.shape[0]
    nf = D_FF // MOE_TF
    per = MOE_ROWS // MOE_SUB
    assert per == 4

    def x_spec(s):
        return pl.BlockSpec((MOE_SUB, d),
                            lambda b, f, blk, bexp, nsub, last: (jnp.minimum(per * blk[b] + s, last[b]), 0))

    wsel = lambda b, f, blk, bexp, nsub, last: bexp[b]
    in_specs = [x_spec(0), x_spec(1), x_spec(2), x_spec(3),
                pl.BlockSpec((None, d, MOE_TF), lambda b, f, *p: (wsel(b, f, *p), 0, f)),
                pl.BlockSpec((None, d, MOE_TF), lambda b, f, *p: (wsel(b, f, *p), 0, nf + f)),
                pl.BlockSpec((None, 1, MOE_TF), lambda b, f, *p: (wsel(b, f, *p), 0, f)),
                pl.BlockSpec((None, 1, MOE_TF), lambda b, f, *p: (wsel(b, f, *p), 0, nf + f)),
                pl.BlockSpec((None, MOE_TF, d), lambda b, f, *p: (wsel(b, f, *p), f, 0)),
                pl.BlockSpec((None, 1, d), lambda b, f, *p: (wsel(b, f, *p), 0, 0))]
    return pl.pallas_call(
        _expert_kernel,
        grid_spec=pltpu.PrefetchScalarGridSpec(
            num_scalar_prefetch=4, grid=(nb, nf), in_specs=in_specs,
            out_specs=pl.BlockSpec((MOE_ROWS, d), lambda b, f, blk, bexp, nsub, last: (blk[b], 0)),
            scratch_shapes=[pltpu.VMEM((MOE_ROWS, d), BF16), pltpu.VMEM((d, MOE_TF), BF16),
                            pltpu.VMEM((d, MOE_TF), BF16), pltpu.VMEM((MOE_TF, d), BF16)]),
        out_shape=jax.ShapeDtypeStruct((n_rows, d), F32),
        compiler_params=_cp(("arbitrary", "arbitrary")),
        name="moe_experts",
    )(blk, bexp, nsub, last_sub, xs, xs, xs, xs, w_in, w_in,
      b_in.reshape(N_EXPERTS, 1, 2 * D_FF), b_in.reshape(N_EXPERTS, 1, 2 * D_FF), w_out,
      b_out.reshape(N_EXPERTS, 1, d))


def _combine_kernel(dest_ref, x_ref, mf_ref, g_ref, ys_ref, o_ref, h_ref, buf, dest_smem, sem_idx, sem):
    _load_dest(dest_ref, dest_smem, sem_idx)
    _for_each_assignment(
        dest_smem, lambda r, k, src: _row_copy(ys_ref, src, buf.at[k], r, sem).start(priority=k % 2))
    _drain_rows(_row_copy(ys_ref, 0, buf.at[0], 0, sem), dest_smem)
    gates = mf_ref[...]
    acc = x_ref[...]
    for k in range(TOP_K):
        acc = acc + gates[:, k:k + 1] * buf[k]
    o_ref[...] = acc
    h = acc * lax.rsqrt(jnp.mean(acc * acc, axis=-1, keepdims=True) + EPS) * g_ref[...]
    h_ref[...] = h.astype(h_ref.dtype)


def combine(x, mf, dest2d, ys, gamma_next):
    m, d = x.shape
    tm = TOK_TILE
    drows = tm * TOP_K // LANES
    return pl.pallas_call(
        _combine_kernel,
        grid=(m // tm,),
        in_specs=[pl.BlockSpec((drows, LANES), lambda i: (i, 0)),
                  pl.BlockSpec((tm, d), lambda i: (i, 0)),
                  pl.BlockSpec((tm, LANES), lambda i: (i, 0)),
                  pl.BlockSpec((1, d), lambda i: (0, 0)),
                  pl.BlockSpec(memory_space=pl.ANY)],
        out_specs=[pl.BlockSpec((tm, d), lambda i: (i, 0)), pl.BlockSpec((tm, d), lambda i: (i, 0))],
        out_shape=[jax.ShapeDtypeStruct((m, d), F32), jax.ShapeDtypeStruct((m, d), BF16)],
        scratch_shapes=[pltpu.VMEM((TOP_K, tm, d), F32), pltpu.SMEM((drows, LANES), I32),
                        pltpu.SemaphoreType.DMA, pltpu.SemaphoreType.DMA],
        compiler_params=_cp(("arbitrary",)),
        name="moe_combine",
    )(dest2d, x, mf, gamma_next.reshape(1, d), ys)


def routed_experts_residual(x, gamma, w_router, b_router, w_exp_in, b_exp_in, w_exp_out, b_exp_out, gamma_next):
    m, d = x.shape
    mi, mf, cnt = router(x, gamma, w_router, b_router)
    idx = mi[:, :TOP_K]
    rank = mi[:, TOP_K:2 * TOP_K]
    counts = cnt[0, :N_EXPERTS].astype(I32)
    padded = (counts + MOE_ROWS - 1) // MOE_ROWS * MOE_ROWS
    pad_end = jnp.cumsum(padded)
    pad_start = pad_end - padded
    onehot = idx[:, :, None] == jnp.arange(N_EXPERTS, dtype=I32)[None, None, :]
    dest = jnp.sum(jnp.where(onehot, pad_start[None, None, :], 0), axis=-1) + rank
    dest2d = dest.reshape(m * TOP_K // LANES, LANES)
    n_blocks = (m * TOP_K + N_EXPERTS * (MOE_ROWS - 1) + MOE_ROWS - 1) // MOE_ROWS
    used = pad_end[-1] // MOE_ROWS
    bids = jnp.arange(n_blocks, dtype=I32)
    blk = jnp.minimum(bids, jnp.maximum(used - 1, 0))
    bexp = jnp.minimum(jnp.sum((pad_end[None, :] <= (blk * MOE_ROWS)[:, None]).astype(I32), axis=1), N_EXPERTS - 1)
    fill_end = pad_start + counts
    valid_rows = jnp.clip(fill_end[bexp] - blk * MOE_ROWS, 0, MOE_ROWS)
    nsub = jnp.where(bids < used, (valid_rows + MOE_SUB - 1) // MOE_SUB, 0).astype(I32)
    per = MOE_ROWS // MOE_SUB
    last_sub = (per * blk + jnp.maximum(nsub, 1) - 1).astype(I32)
    n_rows = n_blocks * MOE_ROWS
    tail_sub = jnp.minimum(fill_end // MOE_SUB, n_rows // MOE_SUB - 1).astype(I32)
    xs = zero_partial_subblocks(tail_sub, n_rows, d)
    xs = dispatch(x, gamma, dest2d, xs)
    ys = experts(xs, w_exp_in, b_exp_in, w_exp_out, b_exp_out, blk, bexp, nsub, last_sub)
    return combine(x, mf, dest2d, ys, gamma_next)


def _ep_plain(dots, extras):
    return dots[0]


def _ep_sigmoid(dots, extras):
    return _sigmoid(dots[0])


def _ep_merge(dots, extras):
    return extras[0] * dots[0] + extras[1] * dots[1]


def _ep_residual(dots, extras):
    return extras[0] + dots[0]


def _ep_ple(dots, extras):
    return extras[0] + _sigmoid(dots[0]) * dots[1]


def kernel(x_prompt, x_sample, cache_k, cache_v, state_conv, state_ssm, p_prompt, p_sample, w_in, sinks, conv_w, a_log, dt_bias, dn_norm, w_attn_branch, w_dn_branch, w_out, w_router, b_router, w_exp_in, b_exp_in, w_exp_out, b_exp_out, w_ple_proj, w_ple_gate, norm_mix, norm_moe, norm_ple, norm_final):
    depth = w_in.shape[0]
    bp, seq, d = x_prompt.shape
    bs, t_new, _ = x_sample.shape
    w_cache = cache_k.shape[2]
    mp, ms = bp * seq, bs * t_new
    x = (x_prompt.reshape(mp, d), x_sample.reshape(ms, d))
    p_prompt, p_sample = p_prompt.reshape(depth, mp, -1), p_sample.reshape(depth, ms, -1)
    m = mp + ms
    tm_split = 512
    assert mp % tm_split == 0 and ms % tm_split == 0
    tm_big = 1536 if m % 1536 == 0 else 512
    outs = {k: [] for k in ("kp", "vp", "cp", "sp", "ks", "vs", "cs", "ss")}
    for i in range(depth):
        h0 = rms_rows(x, norm_mix[i], BF16, tm_split)
        proj = fused_mm([(h0, w_in[i], 0)], [], _ep_plain, COL_BA, F32, tm_big, 512, "proj_main")
        w_ba = jnp.zeros((d, LANES), F32).at[:, :2 * N_DN_HEADS].set(w_in[i][:, COL_BA:COL_GATES])
        ba = fused_mm([(h0, w_ba, 0)], [], _ep_plain, LANES, F32, tm_big, LANES, "proj_ba")
        gates = fused_mm([(h0, w_in[i][:, COL_GATES:], 0)], [], _ep_sigmoid, 2 * d, F32, tm_big, 1024, "proj_gates")

        o_a_p = swa_prompt(proj, sinks[i], bp, seq)
        ps = proj[mp:]
        q_s = ps[:, :ATT_Q].reshape(bs, t_new, N_KV_HEADS, GROUP, HEAD_DIM).transpose(0, 2, 3, 1, 4)
        q_s = q_s.reshape(bs, N_KV_HEADS, GROUP * t_new, HEAD_DIM)
        k_s = ps[:, ATT_Q:ATT_Q + ATT_KV].reshape(bs, t_new, N_KV_HEADS, HEAD_DIM)
        v_s = ps[:, ATT_Q + ATT_KV:COL_QKV_D].reshape(bs, t_new, N_KV_HEADS, HEAD_DIM)
        o_a_s = swa_sample(q_s, k_s.transpose(0, 2, 1, 3), v_s.transpose(0, 2, 1, 3),
                           cache_k[i].transpose(0, 2, 1, 3), cache_v[i].transpose(0, 2, 1, 3), sinks[i])
        o_a_s = o_a_s.reshape(bs, N_KV_HEADS, GROUP, t_new, HEAD_DIM).transpose(0, 3, 1, 2, 4).reshape(ms, ATT_Q)
        o_a = (o_a_p, o_a_s.astype(BF16))
        tail = lambda n, c0, c1: jnp.stack(
            [lax.slice(proj, ((b + 1) * seq - n, c0), ((b + 1) * seq, c1)) for b in range(bp)])
        outs["kp"].append(tail(WINDOW, ATT_Q, ATT_Q + ATT_KV).reshape(bp, WINDOW, N_KV_HEADS, HEAD_DIM))
        outs["vp"].append(tail(WINDOW, ATT_Q + ATT_KV, COL_QKV_D).reshape(bp, WINDOW, N_KV_HEADS, HEAD_DIM))
        outs["ks"].append(jnp.concatenate([cache_k[i], k_s], axis=1)[:, -w_cache:])
        outs["vs"].append(jnp.concatenate([cache_v[i], v_s], axis=1)[:, -w_cache:])

        o_d_p, s_p = gdn_prompt(proj, ba, conv_w[i], a_log[i], dt_bias[i], dn_norm[i], bp, seq)
        qkv_s = ps[:, COL_QKV_D:COL_Z].reshape(bs, t_new, CONV_DIM)
        rows = SUBLANES
        pad_rows = rows - (CONV_W - 1) - t_new
        xp_s = jnp.concatenate([state_conv[i], qkv_s, jnp.zeros((bs, pad_rows, CONV_DIM), F32)], axis=1)
        tok_pad = lambda a: jnp.pad(a.reshape(bs, t_new, -1), ((0, 0), (0, rows - t_new), (0, 0)))
        o_d_s, s_s = gdn_sample(xp_s, tok_pad(ps[:, COL_Z:COL_BA]), tok_pad(ba[mp:]), conv_w[i], a_log[i],
                                dt_bias[i], dn_norm[i], state_ssm[i], t_new)
        o_d = (o_d_p, o_d_s[:, :t_new].reshape(ms, DN_V))
        outs["cp"].append(tail(CONV_W - 1, COL_QKV_D, COL_Z))
        outs["cs"].append(jnp.concatenate([state_conv[i], qkv_s], axis=1)[:, -(CONV_W - 1):])
        outs["sp"].append(s_p)
        outs["ss"].append(s_s)

        merged = fused_mm([(o_a, w_attn_branch[i], 0), (o_d, w_dn_branch[i], 0)], [gates, ColOff(gates, d)],
                          _ep_merge, d, BF16, tm_split, 1024, "branch_merge")
        x = fused_mm([(merged, w_out[i], 0)], [x], _ep_residual, d, F32, tm_split, 1024, "out_proj")

        x, hp = routed_experts_residual(x, norm_moe[i], w_router[i], b_router[i], w_exp_in[i], b_exp_in[i],
                                        w_exp_out[i], b_exp_out[i], norm_ple[i])

        x = fused_mm([(hp, w_ple_gate[i], 0), ((p_prompt[i], p_sample[i]), w_ple_proj[i], 0)], [x], _ep_ple, d, F32,
                     tm_split, 1024, "ple")
    y_p, y_s = rms_rows(x, norm_final, F32, tm_split, split_rows=(mp, ms))
    st = lambda k: jnp.stack(outs[k])
    return (y_p.reshape(bp, seq, d), y_s.reshape(bs, t_new, d), st("kp"), st("vp"), st("cp"), st("sp"),
            st("ks"), st("vs"), st("cs"), st("ss"))
```
